```python
import math
import jax, jax.numpy as jnp
from jax import lax
import numpy as np

D_MODEL = 1024
BATCH = 2
SEQ = 16384
DEPTH = 4

HEAD_DIM = 64
Q_BLOCK = 128
ROPE_THETA = 10000.0
NORM_EPS = 1e-6
A_HEADS = 4
A_V_DIM = 2 * HEAD_DIM
B_HEADS = 8
B_PATTERNS = ((128, 1), (512, 4), (2048, 16))
C_HEADS = 4
C_NOPE_DIM = 64
C_ROPE_DIM = 32
C_V_DIM = 128
C_Q_RANK = 384
C_KV_RANK = 256
D_HEADS = 4
D_HEAD_DIM = 128
MIX_WIDTH = A_HEADS * A_V_DIM + B_HEADS * HEAD_DIM
EVEN_WIDTHS = (A_HEADS * 2 * HEAD_DIM, A_HEADS * 2 * HEAD_DIM, A_HEADS * A_V_DIM,
               B_HEADS * HEAD_DIM, B_HEADS * HEAD_DIM, B_HEADS * HEAD_DIM)
ODD_WIDTHS = (C_Q_RANK, C_KV_RANK, C_ROPE_DIM,
              D_HEADS * D_HEAD_DIM, D_HEADS * D_HEAD_DIM, D_HEADS * D_HEAD_DIM)
EVEN_IN = 3072
ODD_IN = 2208
D_FF = 2816
CONV_WIDTH = 3
N_EVEN = (DEPTH + 1) // 2
N_ODD = DEPTH // 2

kernel_name = "hybrid_diff_dilated_mla_stickbreak_trunk"


def rms_norm(x, g):
    xf = x.astype(jnp.float32)
    y = xf * lax.rsqrt(jnp.mean(xf * xf, axis=-1, keepdims=True) + NORM_EPS)
    return (y * g.astype(jnp.float32)).astype(x.dtype)


def rope_tables(seq, dim):
    inv_freq = ROPE_THETA ** (-jnp.arange(0, dim, 2, dtype=jnp.float32) / dim)
    ang = jnp.arange(seq, dtype=jnp.float32)[:, None] * inv_freq[None, :]
    return jnp.cos(ang), jnp.sin(ang)


def apply_rope(x, cos, sin):
    half = x.shape[-1] // 2
    shape = (1, x.shape[1]) + (1,) * (x.ndim - 3) + (half,)
    c = cos.reshape(shape).astype(x.dtype)
    s = sin.reshape(shape).astype(x.dtype)
    x1, x2 = x[..., :half], x[..., half:]
    return jnp.concatenate([x1 * c - x2 * s, x2 * c + x1 * s], axis=-1)


def split_cols(p, widths):
    out, start = [], 0
    for w in widths:
        out.append(p[..., start:start + w])
        start += w
    return out


def causal_query_blocks(fn, seq):
    return jnp.concatenate([fn(i * Q_BLOCK, (i + 1) * Q_BLOCK) for i in range(seq // Q_BLOCK)], axis=1)


def causal_mask(q0, kl, strict=False):
    qpos = q0 + jnp.arange(Q_BLOCK)[:, None]
    kpos = jnp.arange(kl)[None, :]
    return kpos < qpos if strict else kpos <= qpos


def masked_exp(s, mask):
    s = jnp.where(mask, s, -jnp.inf)
    e = jnp.exp(s - jnp.max(s, axis=-1, keepdims=True))
    return e, jnp.sum(e, axis=-1)


def reverse_exclusive_cumsum(x):
    lead, n = x.shape[:-1], x.shape[-1]
    nkb = n // Q_BLOCK
    xb = x.reshape(lead + (nkb, Q_BLOCK))
    tri = (jnp.arange(Q_BLOCK)[:, None] > jnp.arange(Q_BLOCK)[None, :]).astype(x.dtype)
    tri_b = (jnp.arange(nkb)[:, None] > jnp.arange(nkb)[None, :]).astype(x.dtype)
    within = jnp.einsum('...cm,mj->...cj', xb, tri)
    later = jnp.einsum('...m,mc->...c', jnp.sum(xb, axis=-1), tri_b)
    return (within + later[..., None]).reshape(x.shape)


def diff_attention(q, k, v, lam):
    scale = HEAD_DIM ** -0.5

    def block(q0, kl):
        s = jnp.einsum('bqhcd,bkhcd->bhcqk', q[:, q0:kl] * scale, k[:, :kl],
                       preferred_element_type=jnp.float32)
        e, den = masked_exp(s, causal_mask(q0, kl))
        o = jnp.einsum('bhcqk,bkhe->bqhce', e.astype(v.dtype), v[:, :kl])
        o = (o / jnp.transpose(den, (0, 3, 1, 2))[..., None]).astype(v.dtype)
        return o[:, :, :, 0] - lam * o[:, :, :, 1]

    return causal_query_blocks(block, q.shape[1])


def dilated_branch(q, k, v, window, dil):
    b, s, h, dh = q.shape
    qb = Q_BLOCK // dil
    nb = s // Q_BLOCK
    n_back = window // dil
    n_prev = -(-n_back // qb)
    band = (n_prev + 1) * qb
    scale = dh ** -0.5

    def to_blocks(t):
        return t.reshape(b, nb, qb, dil, h, dh)

    def to_band(t):
        tp = jnp.pad(to_blocks(t), ((0, 0), (n_prev, 0), (0, 0), (0, 0), (0, 0), (0, 0)))
        return jnp.concatenate([tp[:, c:c + nb] for c in range(n_prev + 1)], axis=2)

    kb, vb = to_band(k), to_band(v)
    qi = jnp.arange(qb)[:, None]
    kj = jnp.arange(band)[None, :]
    dist = qi + n_prev * qb - kj
    kpos = (jnp.arange(nb)[:, None, None] - n_prev) * qb + kj[None]
    valid = (dist >= 0) & (dist <= n_back) & (kpos >= 0)
    sc = jnp.einsum('bnqrhd,bnkrhd->bnrhqk', to_blocks(q) * scale, kb,
                    preferred_element_type=jnp.float32)
    sc = jnp.where(valid[None, :, None, None], sc, -jnp.inf)
    m = jnp.max(sc, axis=-1, keepdims=True)
    e = jnp.exp(sc - m)
    den = jnp.sum(e, axis=-1, keepdims=True)
    o = jnp.einsum('bnrhqk,bnkrhd->bnqrhd', (e / den).astype(v.dtype), vb).reshape(b, s, h, dh)
    lse = jnp.transpose((m + jnp.log(den))[..., 0], (0, 1, 4, 2, 3)).reshape(b, s, h)
    return o, lse


def dilated_attention(q, k, v):
    outs, lses = [], []
    for window, dil in B_PATTERNS:
        o, lse = dilated_branch(q, k, v, window, dil)
        outs.append(o)
        lses.append(lse)
    alpha = jax.nn.softmax(jnp.stack(lses, axis=0), axis=0)
    return jnp.einsum('gbsh,gbshd->bshd', alpha.astype(v.dtype), jnp.stack(outs, axis=0))


def mla_attention(q, k, v):
    scale = (C_NOPE_DIM + C_ROPE_DIM) ** -0.5

    def block(q0, kl):
        s = jnp.einsum('bqhd,bkhd->bhqk', q[:, q0:kl] * scale, k[:, :kl],
                       preferred_element_type=jnp.float32)
        e, den = masked_exp(s, causal_mask(q0, kl))
        o = jnp.einsum('bhqk,bkhd->bqhd', e.astype(v.dtype), v[:, :kl])
        return (o / jnp.transpose(den, (0, 2, 1))[..., None]).astype(v.dtype)

    return causal_query_blocks(block, q.shape[1])


def stick_breaking_attention(q, k, v):
    scale = D_HEAD_DIM ** -0.5

    def block(q0, kl):
        z = jnp.einsum('bqhd,bkhd->bhqk', q[:, q0:kl] * scale, k[:, :kl],
                       preferred_element_type=jnp.float32)
        mask = causal_mask(q0, kl, strict=True)
        log_keep = jnp.where(mask, jax.nn.log_sigmoid(-z), 0.0)
        a = jnp.where(mask, jnp.exp(z + log_keep + reverse_exclusive_cumsum(log_keep)), 0.0)
        return jnp.einsum('bhqk,bkhd->bqhd', a.astype(v.dtype), v[:, :kl])

    return causal_query_blocks(block, q.shape[1])


def even_mixer(h, w_in, lq1, lk1, lq2, lk2, subln_g, w_out, lam_init, cos, sin):
    b, s, _ = h.shape
    a_q, a_k, a_v, b_q, b_k, b_v = split_cols(h @ w_in, EVEN_WIDTHS)
    a_q = apply_rope(a_q.reshape(b, s, A_HEADS, 2, HEAD_DIM), cos, sin)
    a_k = apply_rope(a_k.reshape(b, s, A_HEADS, 2, HEAD_DIM), cos, sin)
    a_v = a_v.reshape(b, s, A_HEADS, A_V_DIM)
    f32 = jnp.float32
    lam = (jnp.exp(jnp.sum(lq1.astype(f32) * lk1.astype(f32)))
           - jnp.exp(jnp.sum(lq2.astype(f32) * lk2.astype(f32))) + lam_init)
    o_a = diff_attention(a_q, a_k, a_v, lam)
    o_a = rms_norm(o_a, subln_g) * (1.0 - lam_init)
    b_q = apply_rope(b_q.reshape(b, s, B_HEADS, HEAD_DIM), cos, sin)
    b_k = apply_rope(b_k.reshape(b, s, B_HEADS, HEAD_DIM), cos, sin)
    o_b = dilated_attention(b_q, b_k, b_v.reshape(b, s, B_HEADS, HEAD_DIM))
    o = jnp.concatenate([o_a.reshape(b, s, -1), o_b.reshape(b, s, -1)], axis=-1)
    return o @ w_out


def odd_mixer(h, w_in, q_norm_g, w_uq, kv_norm_g, w_ukv, w_out, cos_c, sin_c):
    b, s, _ = h.shape
    c_q, c_kv, c_kr, d_q, d_k, d_v = split_cols(h @ w_in, ODD_WIDTHS)
    q = (rms_norm(c_q, q_norm_g) @ w_uq).reshape(b, s, C_HEADS, C_NOPE_DIM + C_ROPE_DIM)
    q = jnp.concatenate([q[..., :C_NOPE_DIM], apply_rope(q[..., C_NOPE_DIM:], cos_c, sin_c)], axis=-1)
    kv = (rms_norm(c_kv, kv_norm_g) @ w_ukv).reshape(b, s, C_HEADS, C_NOPE_DIM + C_V_DIM)
    k_nope, v_c = kv[..., :C_NOPE_DIM], kv[..., C_NOPE_DIM:]
    k_rope = apply_rope(c_kr, cos_c, sin_c)
    k = jnp.concatenate([k_nope, jnp.broadcast_to(k_rope[:, :, None], (b, s, C_HEADS, C_ROPE_DIM))], axis=-1)
    o_c = mla_attention(q, k, v_c)
    o_d = stick_breaking_attention(d_q.reshape(b, s, D_HEADS, D_HEAD_DIM),
                                   d_k.reshape(b, s, D_HEADS, D_HEAD_DIM),
                                   d_v.reshape(b, s, D_HEADS, D_HEAD_DIM))
    o = jnp.concatenate([o_c.reshape(b, s, -1), o_d.reshape(b, s, -1)], axis=-1)
    return o @ w_out


def conv_ffn(h, w_up, conv_w, conv_b, w_down):
    gate, val = jnp.split(h @ w_up, 2, axis=-1)
    gate = lax.conv_general_dilated(
        gate, conv_w[:, None, :], window_strides=(1,), padding=((CONV_WIDTH - 1, 0),),
        dimension_numbers=('NWC', 'WIO', 'NWC'), feature_group_count=D_FF) + conv_b
    return (jax.nn.silu(gate) * val) @ w_down


def setup_inputs(seed: int = 0) -> dict:
    key = jax.random.key(seed)
    ks = iter(jax.random.split(key, 32))

    def dense(shape, fan_in):
        return jax.random.normal(next(ks), shape, jnp.float32) * (fan_in ** -0.5)

    def gain(shape):
        return 1.0 + 0.02 * jax.random.normal(next(ks), shape, jnp.float32)

    def small(shape, std):
        return std * jax.random.normal(next(ks), shape, jnp.float32)

    return {
        "x": jax.random.normal(next(ks), (BATCH, SEQ, D_MODEL), jnp.float32),
        "mix_norm_g": gain((DEPTH, D_MODEL)),
        "ffn_norm_g": gain((DEPTH, D_MODEL)),
        "final_norm_g": gain((D_MODEL,)),
        "ab_w_in": dense((N_EVEN, D_MODEL, EVEN_IN), D_MODEL),
        "a_lambda_q1": small((N_EVEN, HEAD_DIM), 0.1),
        "a_lambda_k1": small((N_EVEN, HEAD_DIM), 0.1),
        "a_lambda_q2": small((N_EVEN, HEAD_DIM), 0.1),
        "a_lambda_k2": small((N_EVEN, HEAD_DIM), 0.1),
        "a_subln_g": gain((N_EVEN, A_V_DIM)),
        "ab_w_out": dense((N_EVEN, MIX_WIDTH, D_MODEL), MIX_WIDTH),
        "cd_w_in": dense((N_ODD, D_MODEL, ODD_IN), D_MODEL),
        "c_q_norm_g": gain((N_ODD, C_Q_RANK)),
        "c_w_uq": dense((N_ODD, C_Q_RANK, C_HEADS * (C_NOPE_DIM + C_ROPE_DIM)), C_Q_RANK),
        "c_kv_norm_g": gain((N_ODD, C_KV_RANK)),
        "c_w_ukv": dense((N_ODD, C_KV_RANK, C_HEADS * (C_NOPE_DIM + C_V_DIM)), C_KV_RANK),
        "cd_w_out": dense((N_ODD, MIX_WIDTH, D_MODEL), MIX_WIDTH),
        "ffn_w_up": dense((DEPTH, D_MODEL, 2 * D_FF), D_MODEL),
        "ffn_conv_w": dense((DEPTH, CONV_WIDTH, D_FF), CONV_WIDTH),
        "ffn_conv_b": small((DEPTH, D_FF), 0.01),
        "ffn_w_down": dense((DEPTH, D_FF, D_MODEL), D_FF),
    }


def reference(x, mix_norm_g, ffn_norm_g, final_norm_g, ab_w_in, a_lambda_q1, a_lambda_k1,
              a_lambda_q2, a_lambda_k2, a_subln_g, ab_w_out, cd_w_in, c_q_norm_g, c_w_uq,
              c_kv_norm_g, c_w_ukv, cd_w_out, ffn_w_up, ffn_conv_w, ffn_conv_b, ffn_w_down):
    seq = x.shape[1]
    cos, sin = rope_tables(seq, HEAD_DIM)
    cos_c, sin_c = rope_tables(seq, C_ROPE_DIM)
    for layer in range(DEPTH):
        h = rms_norm(x, mix_norm_g[layer])
        j = layer // 2
        if layer % 2 == 0:
            lam_init = 0.8 - 0.6 * math.exp(-0.3 * layer)
            y = even_mixer(h, ab_w_in[j], a_lambda_q1[j], a_lambda_k1[j], a_lambda_q2[j],
                           a_lambda_k2[j], a_subln_g[j], ab_w_out[j], lam_init, cos, sin)
        else:
            y = odd_mixer(h, cd_w_in[j], c_q_norm_g[j], c_w_uq[j], c_kv_norm_g[j], c_w_ukv[j],
                          cd_w_out[j], cos_c, sin_c)
        x = x + y
        x = x + conv_ffn(rms_norm(x, ffn_norm_g[layer]), ffn_w_up[layer], ffn_conv_w[layer],
                         ffn_conv_b[layer], ffn_w_down[layer])
    return rms_norm(x, final_norm_g)
```

```python
import functools
import math

import numpy as np
import jax
import jax.numpy as jnp
from jax import lax
from jax.experimental import pallas as pl
from jax.experimental.pallas import tpu as pltpu

F32 = jnp.float32
BF16 = jnp.bfloat16

D_MODEL = 1024
HEAD_DIM = 64
ROPE_THETA = 10000.0
NORM_EPS = 1e-6
A_HEADS = 4
B_PATTERNS = ((128, 1), (512, 4), (2048, 16))
B_BACK = 128
C_HEADS = 4
C_NOPE_DIM = 64
C_ROPE_DIM = 32
C_V_DIM = 128
C_Q_RANK = 384
C_KV_RANK = 256
D_HEADS = 4
D_HEAD_DIM = 128
D_FF = 2816
HALF = 512

LOG2E = 1.4426950408889634
NEG = -1e30
LANES = 128
VMEM_LIMIT = 56 * 1024 * 1024

TOK_TILE = 512
Q_TILE = 512
KV_TILE = 512
SB_KV_TILE = 256
FF_CHUNK = 256
DIL_TILE = 2048
DIL_SUB = 256


def _nt(a, b):
    return lax.dot_general(a, b, (((1,), (1,)), ((), ())), preferred_element_type=F32)


def _mm(a, b):
    return jnp.dot(a, b, preferred_element_type=F32)


def _rms(x, g):
    return x * lax.rsqrt(jnp.mean(x * x, axis=-1, keepdims=True) + NORM_EPS) * g


def _params(*sem):
    return pltpu.CompilerParams(dimension_semantics=sem, vmem_limit_bytes=VMEM_LIMIT)


def _const_spec(shape):
    n = len(shape)
    return pl.BlockSpec(shape, lambda *_: (0,) * n, pipeline_mode=pl.Buffered(1))


def _rope(blk, cos, sin):
    return blk * cos + pltpu.roll(blk, 64, 1) * sin


def _store_vt(vt_ref, vt, tkv):
    for t in range(vt.shape[1] // tkv):
        vt_ref[0, t] = vt[:, t * tkv:(t + 1) * tkv].astype(BF16)


def _proj_even_kernel(x_ref, g_ref, wm_ref, wvt_ref, cos_ref, sin_ref,
                      qa_ref, ka_ref, vat_ref, qb_ref, kb_ref, vb_ref):
    h = _rms(x_ref[0], g_ref[...]).astype(BF16)
    p = _mm(h, wm_ref[...])
    cos, sin = cos_ref[...], sin_ref[...]
    qscale = HEAD_DIM ** -0.5 * LOG2E
    for b in range(HALF // LANES):
        c = slice(b * LANES, (b + 1) * LANES)
        qa_ref[0, :, c] = (_rope(p[:, c], cos, sin) * qscale).astype(BF16)
        ka_ref[0, :, c] = _rope(p[:, HALF + b * LANES:HALF + (b + 1) * LANES], cos, sin).astype(BF16)
        qb_ref[0, :, c] = _rope(p[:, 2 * HALF + b * LANES:2 * HALF + (b + 1) * LANES], cos, sin) * qscale
        kb_ref[0, :, c] = _rope(p[:, 3 * HALF + b * LANES:3 * HALF + (b + 1) * LANES], cos, sin)
    vb_ref[0] = p[:, 4 * HALF:5 * HALF]
    _store_vt(vat_ref, _nt(wvt_ref[...], h), KV_TILE)


def _proj_odd_kernel(x_ref, g_ref, wm_ref, wdvt_ref, qg_ref, wuq_ref, kvg_ref, wukvk_ref, wukvvt_ref,
                     cos_ref, sin_ref, qc_ref, kc_ref, vct_ref, qd_ref, kd_ref, vdt_ref):
    h = _rms(x_ref[0], g_ref[...]).astype(BF16)
    p = _mm(h, wm_ref[...])
    cos, sin = cos_ref[...], sin_ref[...]
    cqn = _rms(p[:, :C_Q_RANK], qg_ref[...]).astype(BF16)
    q = _mm(cqn, wuq_ref[...])
    o_kv = C_Q_RANK
    ckvn = _rms(p[:, o_kv:o_kv + C_KV_RANK], kvg_ref[...]).astype(BF16)
    kn = _mm(ckvn, wukvk_ref[...])
    _store_vt(vct_ref, _nt(wukvvt_ref[...], ckvn), KV_TILE)
    o_kr = o_kv + C_KV_RANK
    kr = _rope(p[:, o_kr:o_kr + LANES], cos, sin)
    cscale = (C_NOPE_DIM + C_ROPE_DIM) ** -0.5 * LOG2E
    for b in range(C_HEADS):
        c = slice(b * LANES, (b + 1) * LANES)
        qc_ref[0, :, c] = (_rope(q[:, c], cos, sin) * cscale).astype(BF16)
        kc_ref[0, :, c] = (kn[:, c] + kr).astype(BF16)
    o_dq = o_kr + LANES
    qd_ref[0] = (p[:, o_dq:o_dq + HALF] * (D_HEAD_DIM ** -0.5 * LOG2E)).astype(BF16)
    kd_ref[0] = p[:, o_dq + HALF:o_dq + 2 * HALF].astype(BF16)
    _store_vt(vdt_ref, _nt(wdvt_ref[...], h), SB_KV_TILE)


def _flash_kernel(*refs, n_maps, tq, tk, lam_init):
    if n_maps == 2:
        q_ref, k_ref, vt_ref, lq1_ref, lk1_ref, lq2_ref, lk2_ref, g_ref, o_ref, acc_ref, m_ref, l_ref = refs
    else:
        q_ref, k_ref, vt_ref, o_ref, acc_ref, m_ref, l_ref = refs
    i = pl.program_id(2)
    q = q_ref[0]
    if n_maps == 2:
        lane = lax.broadcasted_iota(jnp.int32, (1, LANES), 1)
        zero = jnp.zeros_like(q)
        qs = [jnp.where((lane & 32) == 0, q, zero), jnp.where((lane & 32) != 0, q, zero)]
    else:
        qs = [q]
    m_ref[...] = jnp.full(m_ref.shape, NEG, F32)
    l_ref[...] = jnp.zeros(l_ref.shape, F32)
    acc_ref[...] = jnp.zeros(acc_ref.shape, F32)

    def tile(j, mask):
        k = k_ref[0, j]
        vt = vt_ref[0, j]
        for c in range(n_maps):
            s = _nt(k, qs[c])
            if mask is not None:
                s = jnp.where(mask, s, NEG)
            m_prev = m_ref[c]
            m_new = jnp.maximum(m_prev, jnp.max(s, axis=0, keepdims=True))
            alpha = jnp.exp2(m_prev - m_new)
            p = jnp.exp2(s - m_new)
            l_ref[c] = alpha * l_ref[c] + jnp.sum(p, axis=0, keepdims=True)
            acc_ref[c] = alpha * acc_ref[c] + _mm(vt, p.astype(BF16))
            m_ref[c] = m_new

    nsub = tq // tk
    n_full = i * nsub

    def full_body(j, carry):
        tile(j, None)
        return carry

    lax.fori_loop(0, n_full, full_body, 0)
    for jj in range(nsub):
        row = lax.broadcasted_iota(jnp.int32, (tk, tq), 0) + jj * tk
        col = lax.broadcasted_iota(jnp.int32, (tk, tq), 1)
        tile(n_full + jj, row <= col)

    if n_maps == 2:
        o0 = acc_ref[0] / l_ref[0]
        o1 = acc_ref[1] / l_ref[1]
        lam = (jnp.exp(jnp.sum(lq1_ref[...] * lk1_ref[...], axis=-1, keepdims=True))
               - jnp.exp(jnp.sum(lq2_ref[...] * lk2_ref[...], axis=-1, keepdims=True)) + lam_init)
        o = o0 - lam * o1
        o = o * lax.rsqrt(jnp.mean(o * o, axis=0, keepdims=True) + NORM_EPS) * g_ref[...]
        o = o * (1.0 - lam_init)
    else:
        o = acc_ref[0] / l_ref[0]
    o_ref[0] = o.T.astype(o_ref.dtype)


def _flash_attention(q, k, vt, *, n_maps, lam_init=0.0, lam_params=None, subln_g=None):
    bsz, s, width = q.shape
    heads = width // LANES
    tq = min(Q_TILE, s)
    tk = vt.shape[-1]
    assert tq % tk == 0 and s % tq == 0
    k4 = k.reshape(bsz, s // tk, tk, width)
    in_specs = [
        pl.BlockSpec((1, tq, LANES), lambda b, h, i: (b, i, h)),
        pl.BlockSpec((1, s // tk, tk, LANES), lambda b, h, i: (b, 0, 0, h)),
        pl.BlockSpec((1, s // tk, LANES, tk), lambda b, h, i: (b, 0, h, 0)),
    ]
    args = [q, k4, vt]
    if n_maps == 2:
        in_specs += [pl.BlockSpec((1, HEAD_DIM), lambda b, h, i: (0, 0))] * 4
        in_specs += [pl.BlockSpec((LANES, 1), lambda b, h, i: (0, 0))]
        args += [p.reshape(1, HEAD_DIM) for p in lam_params] + [subln_g.reshape(LANES, 1)]
    return pl.pallas_call(
        functools.partial(_flash_kernel, n_maps=n_maps, tq=tq, tk=tk, lam_init=lam_init),
        grid=(bsz, heads, s // tq),
        in_specs=in_specs,
        out_specs=pl.BlockSpec((1, tq, LANES), lambda b, h, i: (b, i, h)),
        out_shape=jax.ShapeDtypeStruct((bsz, s, width), BF16),
        scratch_shapes=[pltpu.VMEM((n_maps, LANES, tq), F32),
                        pltpu.VMEM((n_maps, 1, tq), F32),
                        pltpu.VMEM((n_maps, 1, tq), F32)],
        compiler_params=_params("arbitrary", "arbitrary", "arbitrary"),
        name="flash_diff" if n_maps == 2 else "flash_mla",
    )(*args)


def _stick_kernel(q_ref, k_ref, vt_ref, u_ref, o_ref, acc_ref, r_ref, *, tq, tk):
    i = pl.program_id(2)
    q = q_ref[0]
    u = u_ref[...]
    acc_ref[...] = jnp.zeros(acc_ref.shape, F32)
    r_ref[...] = jnp.zeros(r_ref.shape, F32)

    def tile(j, mask):
        z = _nt(k_ref[0, j], q)
        lk = -(jnp.maximum(z, 0.0) + jnp.log2(1.0 + jnp.exp2(-jnp.abs(z))))
        if mask is not None:
            lk = jnp.where(mask, lk, 0.0)
        la = z + lk + _mm(u, lk.astype(BF16)) + r_ref[...]
        if mask is not None:
            la = jnp.where(mask, la, NEG)
        acc_ref[...] += _mm(vt_ref[0, j], jnp.exp2(la).astype(BF16))
        r_ref[...] += jnp.sum(lk, axis=0, keepdims=True)

    nsub = tq // tk
    n_full = i * nsub
    for jj in reversed(range(nsub)):
        row = lax.broadcasted_iota(jnp.int32, (tk, tq), 0) + jj * tk
        col = lax.broadcasted_iota(jnp.int32, (tk, tq), 1)
        tile(n_full + jj, row < col)

    def full_body(t, carry):
        tile(n_full - 1 - t, None)
        return carry

    lax.fori_loop(0, n_full, full_body, 0)
    o_ref[0] = acc_ref[...].T.astype(o_ref.dtype)


def _stick_attention(q, k, vt):
    bsz, s, width = q.shape
    heads = width // LANES
    tq = min(Q_TILE, s)
    tk = vt.shape[-1]
    assert tq % tk == 0 and s % tq == 0
    k4 = k.reshape(bsz, s // tk, tk, width)
    upper = (np.arange(tk)[None, :] > np.arange(tk)[:, None]).astype(np.float32)
    return pl.pallas_call(
        functools.partial(_stick_kernel, tq=tq, tk=tk),
        grid=(bsz, heads, s // tq),
        in_specs=[
            pl.BlockSpec((1, tq, LANES), lambda b, h, i: (b, i, h)),
            pl.BlockSpec((1, s // tk, tk, LANES), lambda b, h, i: (b, 0, 0, h)),
            pl.BlockSpec((1, s // tk, LANES, tk), lambda b, h, i: (b, 0, h, 0)),
            pl.BlockSpec((tk, tk), lambda b, h, i: (0, 0)),
        ],
        out_specs=pl.BlockSpec((1, tq, LANES), lambda b, h, i: (b, i, h)),
        out_shape=jax.ShapeDtypeStruct((bsz, s, width), BF16),
        scratch_shapes=[pltpu.VMEM((LANES, tq), F32), pltpu.VMEM((1, tq), F32)],
        compiler_params=_params("arbitrary", "arbitrary", "arbitrary"),
        name="stick_breaking",
    )(q, k4, vt, jnp.asarray(upper, BF16))


def _dilated_kernel(q_ref, k_ref, v_ref, o_ref, acc_ref, m_ref, l_ref, *, tile):
    ti = pl.program_id(2)
    lane = lax.broadcasted_iota(jnp.int32, (1, LANES), 1)
    q_lanes = [(lane & 32) == 0, (lane & 32) != 0]
    v_lanes = [lane < HEAD_DIM, lane >= HEAD_DIM]

    for bi, (_, d) in enumerate(B_PATTERNS):
        ts = min(DIL_SUB, tile // d)
        band = ts + B_BACK
        nsub = tile // d // ts

        def body(it, carry, d=d, ts=ts, band=band, nsub=nsub, first=(bi == 0)):
            r = it // nsub
            sub = it % nsub
            row0 = r + d * sub * ts
            m0 = (ti * tile) // d + sub * ts
            ks = jnp.maximum(m0 - B_BACK, 0)
            if d == 1:
                q_idx, k_idx = pl.ds(row0, ts), pl.ds(ks, band)
            else:
                q_idx, k_idx = pl.ds(row0, ts, stride=d), pl.ds(r + d * ks, band, stride=d)
            qs = q_ref[0, q_idx, :]
            kb = k_ref[0, k_idx, :].astype(BF16)
            vb = v_ref[0, k_idx, :]
            dist = ((m0 + lax.broadcasted_iota(jnp.int32, (ts, band), 0))
                    - (ks + lax.broadcasted_iota(jnp.int32, (ts, band), 1)))
            valid = jnp.where(dist >= 0, dist, B_BACK + 1) <= B_BACK
            o_loc, m_loc, l_loc = None, None, None
            for h in range(2):
                s = _nt(jnp.where(q_lanes[h], qs, 0.0).astype(BF16), kb)
                s = jnp.where(valid, s, NEG)
                mh = jnp.max(s, axis=1, keepdims=True)
                p = jnp.exp2(s - mh)
                lh = jnp.sum(p, axis=1, keepdims=True)
                oh = _mm(p.astype(BF16), jnp.where(v_lanes[h], vb, 0.0).astype(BF16))
                if h == 0:
                    o_loc, m_loc, l_loc = oh, mh, lh
                else:
                    o_loc = o_loc + oh
                    m_loc = jnp.where(v_lanes[0], m_loc, mh)
                    l_loc = jnp.where(v_lanes[0], l_loc, lh)
            if first:
                acc_ref[q_idx, :] = o_loc
                m_ref[q_idx, :] = m_loc
                l_ref[q_idx, :] = l_loc
            else:
                m_old = m_ref[q_idx, :]
                m_new = jnp.maximum(m_old, m_loc)
                a_old = jnp.exp2(m_old - m_new)
                a_loc = jnp.exp2(m_loc - m_new)
                acc_ref[q_idx, :] = acc_ref[q_idx, :] * a_old + o_loc * a_loc
                l_ref[q_idx, :] = l_ref[q_idx, :] * a_old + l_loc * a_loc
                m_ref[q_idx, :] = m_new
            return carry

        lax.fori_loop(0, d * nsub, body, 0)

    o_ref[0] = (acc_ref[...] / l_ref[...]).astype(o_ref.dtype)


def _dilated_attention(q, k, v):
    bsz, s, width = q.shape
    tile = min(DIL_TILE, s)
    d_max = B_PATTERNS[-1][1]
    assert s % tile == 0 and (min(DIL_SUB, tile // d_max) + B_BACK) * d_max <= s
    kv_spec = pl.BlockSpec((1, s, LANES), lambda b, h, i: (b, 0, h), pipeline_mode=pl.Buffered(1))
    return pl.pallas_call(
        functools.partial(_dilated_kernel, tile=tile),
        grid=(bsz, width // LANES, s // tile),
        in_specs=[pl.BlockSpec((1, tile, LANES), lambda b, h, i: (b, i, h)), kv_spec, kv_spec],
        out_specs=pl.BlockSpec((1, tile, LANES), lambda b, h, i: (b, i, h)),
        out_shape=jax.ShapeDtypeStruct((bsz, s, width), BF16),
        scratch_shapes=[pltpu.VMEM((tile, LANES), F32)] * 3,
        compiler_params=_params("arbitrary", "arbitrary", "arbitrary"),
        name="dilated",
    )(q, k, v)


def _post_kernel(*refs, tm, final):
    if final:
        (x_ref, o1_ref, o2_ref, wo1_ref, wo2_ref, g_ref, wg_ref, wv_ref, cw_ref, cb_ref, wd_ref, fg_ref,
         out_ref, gs_ref) = refs
    else:
        (x_ref, o1_ref, o2_ref, wo1_ref, wo2_ref, g_ref, wg_ref, wv_ref, cw_ref, cb_ref, wd_ref,
         out_ref, gs_ref) = refs

    @pl.when(pl.program_id(1) == 0)
    def _():
        gs_ref[0:8, :] = jnp.zeros((8, D_FF), F32)

    x1 = x_ref[0] + _mm(o1_ref[0], wo1_ref[...]) + _mm(o2_ref[0], wo2_ref[...])
    h = _rms(x1, g_ref[...]).astype(BF16)
    out_ref[0] = x1
    for c in range(D_FF // FF_CHUNK):
        sl = slice(c * FF_CHUNK, (c + 1) * FF_CHUNK)
        gate = _mm(h, wg_ref[:, sl])
        val = _mm(h, wv_ref[:, sl])
        gs_ref[8:8 + tm, sl] = gate
        conv = (cw_ref[0:1, sl] * gs_ref[6:6 + tm, sl] + cw_ref[1:2, sl] * gs_ref[7:7 + tm, sl]
                + cw_ref[2:3, sl] * gate + cb_ref[:, sl])
        act = conv * jax.nn.sigmoid(conv) * val
        out_ref[0] += _mm(act.astype(BF16), wd_ref[sl, :])
        gs_ref[0:8, sl] = gs_ref[tm:tm + 8, sl]
    if final:
        out_ref[0] = _rms(out_ref[0], fg_ref[...])


def _post(x, o1, o2, w_out, ffn_g, w_up, conv_w, conv_b, w_down, final_g):
    bsz, s, _ = x.shape
    tm = min(TOK_TILE, s)
    final = final_g is not None
    tok = lambda w: pl.BlockSpec((1, tm, w), lambda b, i: (b, i, 0))
    in_specs = [tok(D_MODEL), tok(HALF), tok(HALF),
                _const_spec((HALF, D_MODEL)), _const_spec((HALF, D_MODEL)), _const_spec((1, D_MODEL)),
                _const_spec((D_MODEL, D_FF)), _const_spec((D_MODEL, D_FF)),
                _const_spec((3, D_FF)), _const_spec((1, D_FF)), _const_spec((D_FF, D_MODEL))]
    args = [x, o1, o2, w_out[:HALF].astype(BF16), w_out[HALF:].astype(BF16), ffn_g.reshape(1, D_MODEL),
            w_up[:, :D_FF].astype(BF16), w_up[:, D_FF:].astype(BF16), conv_w, conv_b.reshape(1, D_FF),
            w_down.astype(BF16)]
    if final:
        in_specs.append(_const_spec((1, D_MODEL)))
        args.append(final_g.reshape(1, D_MODEL))
    return pl.pallas_call(
        functools.partial(_post_kernel, tm=tm, final=final),
        grid=(bsz, s // tm),
        in_specs=in_specs,
        out_specs=tok(D_MODEL),
        out_shape=jax.ShapeDtypeStruct(x.shape, F32),
        scratch_shapes=[pltpu.VMEM((tm + 8, D_FF), F32)],
        compiler_params=_params("arbitrary", "arbitrary"),
        name="outproj_mlp",
    )(*args)


def _pair_perm():
    j = np.arange(LANES)
    return (j % 64) // 32 * 64 + j // 64 * 32 + j % 32


def _perm512():
    return np.concatenate([b * LANES + _pair_perm() for b in range(HALF // LANES)])


def _rope_tables(seq):
    def tab(dim):
        inv_freq = ROPE_THETA ** (-jnp.arange(0, dim, 2, dtype=F32) / dim)
        ang = jnp.arange(seq, dtype=F32)[:, None] * inv_freq[None, :]
        return jnp.cos(ang), jnp.sin(ang)
    c, s = tab(HEAD_DIM)
    cos_ab = jnp.concatenate([c, c, c, c], axis=1)
    sin_ab = jnp.concatenate([-s, -s, s, s], axis=1)
    c, s = tab(C_ROPE_DIM)
    one = jnp.ones((seq, 32), F32)
    pad1 = jnp.ones((seq, 16), F32)
    zero = jnp.zeros((seq, 32), F32)
    pad0 = jnp.zeros((seq, 16), F32)
    cos_c = jnp.concatenate([one, c, pad1, one, c, pad1], axis=1)
    sin_c = jnp.concatenate([zero, -s, pad0, zero, s, pad0], axis=1)
    return cos_ab, sin_ab, cos_c, sin_c


def _mla_cols(per_head_src):
    dst, src = [], []
    for hd in range(C_HEADS):
        base = hd * LANES
        nope, rope = per_head_src(hd)
        for n in range(C_NOPE_DIM):
            dst.append(base + (n if n < 32 else 64 + n - 32))
            src.append(nope + n)
        if rope is not None:
            for t in range(C_ROPE_DIM):
                dst.append(base + (32 + t if t < 16 else 96 + t - 16))
                src.append(rope + t)
    return np.array(dst), np.array(src)


def _proj_even(x, g, w_in, cos, sin):
    bsz, s, _ = x.shape
    tm = min(TOK_TILE, s)
    perm = _perm512()
    aq, ak, av, bq, bk, bv = (w_in[:, i * HALF:(i + 1) * HALF] for i in range(6))
    wm = jnp.concatenate([aq[:, perm], ak[:, perm], bq[:, perm], bk[:, perm], bv], axis=1).astype(BF16)
    wvt = av.T.astype(BF16)
    tok = lambda w: pl.BlockSpec((1, tm, w), lambda b, i: (b, i, 0))
    vt_spec = pl.BlockSpec((1, tm // KV_TILE, HALF, KV_TILE), lambda b, i: (b, i, 0, 0))
    tab = pl.BlockSpec((tm, LANES), lambda b, i: (i, 0))
    sd = jax.ShapeDtypeStruct
    return pl.pallas_call(
        _proj_even_kernel,
        grid=(bsz, s // tm),
        in_specs=[tok(D_MODEL), _const_spec((1, D_MODEL)), _const_spec((D_MODEL, 5 * HALF)),
                  _const_spec((HALF, D_MODEL)), tab, tab],
        out_specs=[tok(HALF), tok(HALF), vt_spec, tok(HALF), tok(HALF), tok(HALF)],
        out_shape=[sd((bsz, s, HALF), BF16), sd((bsz, s, HALF), BF16),
                   sd((bsz, s // KV_TILE, HALF, KV_TILE), BF16),
                   sd((bsz, s, HALF), F32), sd((bsz, s, HALF), F32), sd((bsz, s, HALF), F32)],
        compiler_params=_params("arbitrary", "arbitrary"),
        name="proj_even",
    )(x, g.reshape(1, D_MODEL), wm, wvt, cos, sin)


def _proj_odd(x, g, w_in, q_norm_g, w_uq, kv_norm_g, w_ukv, cos, sin):
    bsz, s, _ = x.shape
    tm = min(TOK_TILE, s)
    o_kv, o_kr, o_d = C_Q_RANK, C_Q_RANK + C_KV_RANK, C_Q_RANK + C_KV_RANK + C_ROPE_DIM
    kr_dst = np.concatenate([32 + np.arange(16), 96 + np.arange(16)])
    w_kr = jnp.zeros((D_MODEL, LANES), F32).at[:, kr_dst].set(w_in[:, o_kr:o_kr + C_ROPE_DIM])
    wm = jnp.concatenate([w_in[:, :o_kr], w_kr, w_in[:, o_d:o_d + 2 * HALF]], axis=1).astype(BF16)
    wdvt = w_in[:, o_d + 2 * HALF:o_d + 3 * HALF].T.astype(BF16)
    qk_dim = C_NOPE_DIM + C_ROPE_DIM
    dst, src = _mla_cols(lambda hd: (hd * qk_dim, hd * qk_dim + C_NOPE_DIM))
    wuq = jnp.zeros((C_Q_RANK, HALF), F32).at[:, dst].set(w_uq[:, src]).astype(BF16)
    kv_dim = C_NOPE_DIM + C_V_DIM
    dst, src = _mla_cols(lambda hd: (hd * kv_dim, None))
    wukvk = jnp.zeros((C_KV_RANK, HALF), F32).at[:, dst].set(w_ukv[:, src]).astype(BF16)
    v_cols = np.concatenate([hd * kv_dim + C_NOPE_DIM + np.arange(C_V_DIM) for hd in range(C_HEADS)])
    wukvvt = w_ukv[:, v_cols].T.astype(BF16)
    tok = lambda w: pl.BlockSpec((1, tm, w), lambda b, i: (b, i, 0))
    vt_spec = lambda tkv: pl.BlockSpec((1, tm // tkv, HALF, tkv), lambda b, i: (b, i, 0, 0))
    tab = pl.BlockSpec((tm, LANES), lambda b, i: (i, 0))
    sd = jax.ShapeDtypeStruct
    return pl.pallas_call(
        _proj_odd_kernel,
        grid=(bsz, s // tm),
        in_specs=[tok(D_MODEL), _const_spec((1, D_MODEL)), _const_spec(wm.shape),
                  _const_spec((HALF, D_MODEL)), _const_spec((1, C_Q_RANK)), _const_spec((C_Q_RANK, HALF)),
                  _const_spec((1, C_KV_RANK)), _const_spec((C_KV_RANK, HALF)), _const_spec((HALF, C_KV_RANK)),
                  tab, tab],
        out_specs=[tok(HALF), tok(HALF), vt_spec(KV_TILE), tok(HALF), tok(HALF), vt_spec(SB_KV_TILE)],
        out_shape=[sd((bsz, s, HALF), BF16), sd((bsz, s, HALF), BF16),
                   sd((bsz, s // KV_TILE, HALF, KV_TILE), BF16),
                   sd((bsz, s, HALF), BF16), sd((bsz, s, HALF), BF16),
                   sd((bsz, s // SB_KV_TILE, HALF, SB_KV_TILE), BF16)],
        compiler_params=_params("arbitrary", "arbitrary"),
        name="proj_odd",
    )(x, g.reshape(1, D_MODEL), wm, wdvt, q_norm_g.reshape(1, C_Q_RANK), wuq,
      kv_norm_g.reshape(1, C_KV_RANK), wukvk, wukvvt, cos, sin)


def kernel(x, mix_norm_g, ffn_norm_g, final_norm_g, ab_w_in, a_lambda_q1, a_lambda_k1, a_lambda_q2,
           a_lambda_k2, a_subln_g, ab_w_out, cd_w_in, c_q_norm_g, c_w_uq, c_kv_norm_g, c_w_ukv, cd_w_out,
           ffn_w_up, ffn_conv_w, ffn_conv_b, ffn_w_down):
    depth = mix_norm_g.shape[0]
    seq = x.shape[1]
    cos_ab, sin_ab, cos_c, sin_c = _rope_tables(seq)
    for layer in range(depth):
        j = layer // 2
        if layer % 2 == 0:
            lam_init = 0.8 - 0.6 * math.exp(-0.3 * layer)
            qa, ka, vat, qb, kb, vb = _proj_even(x, mix_norm_g[layer], ab_w_in[j], cos_ab, sin_ab)
            o1 = _flash_attention(qa, ka, vat, n_maps=2, lam_init=lam_init,
                                  lam_params=(a_lambda_q1[j], a_lambda_k1[j], a_lambda_q2[j], a_lambda_k2[j]),
                                  subln_g=a_subln_g[j])
            o2 = _dilated_attention(qb, kb, vb)
            w_out = ab_w_out[j]
        else:
            qc, kc, vct, qd, kd, vdt = _proj_odd(x, mix_norm_g[layer], cd_w_in[j], c_q_norm_g[j], c_w_uq[j],
                                                 c_kv_norm_g[j], c_w_ukv[j], cos_c, sin_c)
            o1 = _flash_attention(qc, kc, vct, n_maps=1)
            o2 = _stick_attention(qd, kd, vdt)
            w_out = cd_w_out[j]
        x = _post(x, o1, o2, w_out, ffn_norm_g[layer], ffn_w_up[layer], ffn_conv_w[layer], ffn_conv_b[layer],
                  ffn_w_down[layer], final_norm_g if layer == depth - 1 else None)
    return x
```

```python
import functools
import math

import numpy as np
import jax
import jax.numpy as jnp
from jax import lax
from jax.experimental import pallas as pl
from jax.experimental.pallas import tpu as pltpu

F32 = jnp.float32
BF16 = jnp.bfloat16

D_MODEL = 1024
HEAD_DIM = 64
ROPE_THETA = 10000.0
NORM_EPS = 1e-6
A_HEADS = 4
B_PATTERNS = ((128, 1), (512, 4), (2048, 16))
B_BACK = 128
C_HEADS = 4
C_NOPE_DIM = 64
C_ROPE_DIM = 32
C_V_DIM = 128
C_Q_RANK = 384
C_KV_RANK = 256
D_HEADS = 4
D_HEAD_DIM = 128
D_FF = 2816
HALF = 512

LOG2E = 1.4426950408889634
NEG = -1e30
LANES = 128
VMEM_LIMIT = 56 * 1024 * 1024

TOK_TILE = 512
Q_TILE = 1024
KV_TILE = 512
SB_Q_TILE = 512
SB_KV_TILE = 256
SB_DEAD = -160.0
FF_CHUNK = 256
DIL_TILE = 2048
DIL_SUB = 256


def _nt(a, b):
    return lax.dot_general(a, b, (((1,), (1,)), ((), ())), preferred_element_type=F32)


def _mm(a, b):
    return jnp.dot(a, b, preferred_element_type=F32)


def _rms(x, g):
    return x * lax.rsqrt(jnp.mean(x * x, axis=-1, keepdims=True) + NORM_EPS) * g


def _params(*sem):
    return pltpu.CompilerParams(dimension_semantics=sem, vmem_limit_bytes=VMEM_LIMIT)


def _const_spec(shape):
    n = len(shape)
    return pl.BlockSpec(shape, lambda *_: (0,) * n, pipeline_mode=pl.Buffered(1))


def _rope(blk, cos, sin):
    return blk * cos + pltpu.roll(blk, 64, 1) * sin


def _store_vt(vt_ref, vt, tkv):
    for t in range(vt.shape[1] // tkv):
        vt_ref[0, t] = vt[:, t * tkv:(t + 1) * tkv].astype(BF16)


def _proj_even_kernel(x_ref, g_ref, wm_ref, wvt_ref, cos_ref, sin_ref,
                      qa_ref, ka_ref, vat_ref, qb_ref, kb_ref, vb_ref):
    h = _rms(x_ref[0], g_ref[...]).astype(BF16)
    p = _mm(h, wm_ref[...])
    cos, sin = cos_ref[...], sin_ref[...]
    qscale = HEAD_DIM ** -0.5 * LOG2E
    for b in range(HALF // LANES):
        c = slice(b * LANES, (b + 1) * LANES)
        qa_ref[0, :, c] = (_rope(p[:, c], cos, sin) * qscale).astype(BF16)
        ka_ref[0, :, c] = _rope(p[:, HALF + b * LANES:HALF + (b + 1) * LANES], cos, sin).astype(BF16)
        qb_ref[0, :, c] = _rope(p[:, 2 * HALF + b * LANES:2 * HALF + (b + 1) * LANES], cos, sin) * qscale
        kb_ref[0, :, c] = _rope(p[:, 3 * HALF + b * LANES:3 * HALF + (b + 1) * LANES], cos, sin)
    vb_ref[0] = p[:, 4 * HALF:5 * HALF]
    _store_vt(vat_ref, _nt(wvt_ref[...], h), KV_TILE)


def _proj_odd_kernel(x_ref, g_ref, wm_ref, wdvt_ref, qg_ref, wuq_ref, kvg_ref, wukvk_ref, wukvvt_ref,
                     cos_ref, sin_ref, qc_ref, kc_ref, vct_ref, qd_ref, kd_ref, vdt_ref):
    h = _rms(x_ref[0], g_ref[...]).astype(BF16)
    p = _mm(h, wm_ref[...])
    cos, sin = cos_ref[...], sin_ref[...]
    cqn = _rms(p[:, :C_Q_RANK], qg_ref[...]).astype(BF16)
    q = _mm(cqn, wuq_ref[...])
    o_kv = C_Q_RANK
    ckvn = _rms(p[:, o_kv:o_kv + C_KV_RANK], kvg_ref[...]).astype(BF16)
    kn = _mm(ckvn, wukvk_ref[...])
    _store_vt(vct_ref, _nt(wukvvt_ref[...], ckvn), KV_TILE)
    o_kr = o_kv + C_KV_RANK
    kr = _rope(p[:, o_kr:o_kr + LANES], cos, sin)
    cscale = (C_NOPE_DIM + C_ROPE_DIM) ** -0.5 * LOG2E
    for b in range(C_HEADS):
        c = slice(b * LANES, (b + 1) * LANES)
        qc_ref[0, :, c] = (_rope(q[:, c], cos, sin) * cscale).astype(BF16)
        kc_ref[0, :, c] = (kn[:, c] + kr).astype(BF16)
    o_dq = o_kr + LANES
    qd_ref[0] = (p[:, o_dq:o_dq + HALF] * (D_HEAD_DIM ** -0.5 * LOG2E)).astype(BF16)
    kd_ref[0] = p[:, o_dq + HALF:o_dq + 2 * HALF].astype(BF16)
    _store_vt(vdt_ref, _nt(wdvt_ref[...], h), SB_KV_TILE)


def _flash_kernel(*refs, n_maps, tq, tk, lam_init):
    if n_maps == 2:
        (q_ref, k_ref, vt_ref, lq1_ref, lk1_ref, lq2_ref, lk2_ref, g_ref, o_ref,
         s_ref, mt_ref, at_ref, m_ref, l_ref, acc_ref) = refs
    else:
        q_ref, k_ref, vt_ref, o_ref, s_ref, mt_ref, at_ref, m_ref, l_ref, acc_ref = refs
    i = pl.program_id(2)
    nblk = tq // tk
    q = q_ref[0]
    if n_maps == 2:
        lane = lax.broadcasted_iota(jnp.int32, (1, LANES), 1)
        zero = jnp.zeros_like(q)
        qm = [jnp.where((lane & 32) == 0, q, zero), jnp.where((lane & 32) != 0, q, zero)]
    else:
        qm = [q]
    chains = [(c, g) for c in range(n_maps) for g in range(nblk)]
    qs = [qm[c][g * tk:(g + 1) * tk] for (c, g) in chains]
    m_ref[...] = jnp.full(m_ref.shape, NEG, F32)
    l_ref[...] = jnp.zeros(l_ref.shape, F32)
    acc_ref[...] = jnp.zeros(acc_ref.shape, F32)
    causal = (lax.broadcasted_iota(jnp.int32, (tk, tk), 0) <= lax.broadcasted_iota(jnp.int32, (tk, tk), 1))

    def stage1(j, slot, active):
        k = k_ref[0, j]
        for ch, masked in active.items():
            s = _nt(k, qs[ch])
            if masked:
                s = jnp.where(causal, s, NEG)
            m_old = m_ref[ch]
            m_new = jnp.maximum(m_old, jnp.max(s, axis=0, keepdims=True))
            s_ref[slot, ch] = s
            mt_ref[slot, ch] = m_new
            at_ref[slot, ch] = jnp.exp2(m_old - m_new)
            m_ref[ch] = m_new

    def stage2(j, slot, active):
        vt = vt_ref[0, j]
        for ch in active:
            p = jnp.exp2(s_ref[slot, ch] - mt_ref[slot, ch])
            alpha = at_ref[slot, ch]
            l_ref[ch] = alpha * l_ref[ch] + jnp.sum(p, axis=0, keepdims=True)
            acc_ref[ch] = alpha * acc_ref[ch] + _mm(vt, p.astype(BF16))

    def run(seq):
        slot = 0
        for idx, (j, active) in enumerate(seq):
            if idx + 1 < len(seq):
                stage1(seq[idx + 1][0], 1 - slot, seq[idx + 1][1])
            stage2(j, slot, active)
            slot = 1 - slot

    full = {ch: False for ch in range(len(chains))}
    diag = [{ch: g == d for ch, (_, g) in enumerate(chains) if g >= d} for d in range(nblk)]
    n_full = i * nblk

    @pl.when(i > 0)
    def _():
        stage1(0, 0, full)

        def pair(u, carry):
            stage1(2 * u + 1, 1, full)
            stage2(2 * u, 0, full)
            stage1(2 * u + 2, 0, full)
            stage2(2 * u + 1, 1, full)
            return carry

        n_pair = (n_full - 1) // 2
        lax.fori_loop(0, n_pair, pair, 0)
        done = 2 * n_pair
        for rem in (1, 2):
            @pl.when(n_full - done == rem)
            def _(rem=rem):
                run([(done + t, full) for t in range(rem)] + [(n_full + d, diag[d]) for d in range(nblk)])

    @pl.when(i == 0)
    def _():
        stage1(0, 0, diag[0])
        run([(d, diag[d]) for d in range(nblk)])

    outs = [jnp.concatenate([acc_ref[c * nblk + g] / l_ref[c * nblk + g] for g in range(nblk)], axis=1)
            for c in range(n_maps)]
    if n_maps == 2:
        lam = (jnp.exp(jnp.sum(lq1_ref[...] * lk1_ref[...], axis=-1, keepdims=True))
               - jnp.exp(jnp.sum(lq2_ref[...] * lk2_ref[...], axis=-1, keepdims=True)) + lam_init)
        o = outs[0] - lam * outs[1]
        o = o * lax.rsqrt(jnp.mean(o * o, axis=0, keepdims=True) + NORM_EPS) * g_ref[...]
        o = o * (1.0 - lam_init)
    else:
        o = outs[0]
    o_ref[0] = o.T.astype(o_ref.dtype)


def _flash_attention(q, k, vt, *, n_maps, lam_init=0.0, lam_params=None, subln_g=None):
    bsz, s, width = q.shape
    heads = width // LANES
    tq = min(Q_TILE, s)
    tk = vt.shape[-1]
    assert tq % tk == 0 and s % tq == 0
    n_chain = n_maps * tq // tk
    k4 = k.reshape(bsz, s // tk, tk, width)
    in_specs = [
        pl.BlockSpec((1, tq, LANES), lambda b, h, i: (b, i, h)),
        pl.BlockSpec((1, s // tk, tk, LANES), lambda b, h, i: (b, 0, 0, h)),
        pl.BlockSpec((1, s // tk, LANES, tk), lambda b, h, i: (b, 0, h, 0)),
    ]
    args = [q, k4, vt]
    if n_maps == 2:
        in_specs += [pl.BlockSpec((1, HEAD_DIM), lambda b, h, i: (0, 0))] * 4
        in_specs += [pl.BlockSpec((LANES, 1), lambda b, h, i: (0, 0))]
        args += [p.reshape(1, HEAD_DIM) for p in lam_params] + [subln_g.reshape(LANES, 1)]
    return pl.pallas_call(
        functools.partial(_flash_kernel, n_maps=n_maps, tq=tq, tk=tk, lam_init=lam_init),
        grid=(bsz, heads, s // tq),
        in_specs=in_specs,
        out_specs=pl.BlockSpec((1, tq, LANES), lambda b, h, i: (b, i, h)),
        out_shape=jax.ShapeDtypeStruct((bsz, s, width), BF16),
        scratch_shapes=[pltpu.VMEM((2, n_chain, tk, tk), F32),
                        pltpu.VMEM((2, n_chain, 1, tk), F32),
                        pltpu.VMEM((2, n_chain, 1, tk), F32),
                        pltpu.VMEM((n_chain, 1, tk), F32),
                        pltpu.VMEM((n_chain, 1, tk), F32),
                        pltpu.VMEM((n_chain, LANES, tk), F32)],
        compiler_params=_params("arbitrary", "arbitrary", "arbitrary"),
        name="flash_diff" if n_maps == 2 else "flash_mla",
    )(*args)


def _stick_kernel(q_ref, k_ref, vt_ref, u_ref, o_ref, acc_ref, r_ref, *, tq, tk):
    i = pl.program_id(2)
    q = q_ref[0]
    u = u_ref[...]
    acc_ref[...] = jnp.zeros(acc_ref.shape, F32)
    r_ref[...] = jnp.zeros(r_ref.shape, F32)

    def tile(j, mask):
        z = _nt(k_ref[0, j], q)
        lk = -(jnp.maximum(z, 0.0) + jnp.log2(1.0 + jnp.exp2(-jnp.abs(z))))
        if mask is not None:
            lk = jnp.where(mask, lk, 0.0)
        la = z + lk + _mm(u, lk.astype(BF16)) + r_ref[...]
        if mask is not None:
            la = jnp.where(mask, la, NEG)
        acc_ref[...] += _mm(vt_ref[0, j], jnp.exp2(la).astype(BF16))
        r_ref[...] += jnp.sum(lk, axis=0, keepdims=True)

    nsub = tq // tk
    n_full = i * nsub
    for jj in reversed(range(nsub)):
        row = lax.broadcasted_iota(jnp.int32, (tk, tq), 0) + jj * tk
        col = lax.broadcasted_iota(jnp.int32, (tk, tq), 1)
        tile(n_full + jj, row < col)

    def live():
        return jnp.max(r_ref[...]) > SB_DEAD

    def cond(carry):
        t, alive = carry
        return jnp.logical_and(t < n_full, alive)

    def body(carry):
        t, _ = carry
        tile(n_full - 1 - t, None)
        return t + 1, live()

    lax.while_loop(cond, body, (jnp.int32(0), live()))
    o_ref[0] = acc_ref[...].T.astype(o_ref.dtype)


def _stick_attention(q, k, vt):
    bsz, s, width = q.shape
    heads = width // LANES
    tq = min(SB_Q_TILE, s)
    tk = vt.shape[-1]
    assert tq % tk == 0 and s % tq == 0
    k4 = k.reshape(bsz, s // tk, tk, width)
    upper = (np.arange(tk)[None, :] > np.arange(tk)[:, None]).astype(np.float32)
    return pl.pallas_call(
        functools.partial(_stick_kernel, tq=tq, tk=tk),
        grid=(bsz, heads, s // tq),
        in_specs=[
            pl.BlockSpec((1, tq, LANES), lambda b, h, i: (b, i, h)),
            pl.BlockSpec((1, s // tk, tk, LANES), lambda b, h, i: (b, 0, 0, h)),
            pl.BlockSpec((1, s // tk, LANES, tk), lambda b, h, i: (b, 0, h, 0)),
            pl.BlockSpec((tk, tk), lambda b, h, i: (0, 0)),
        ],
        out_specs=pl.BlockSpec((1, tq, LANES), lambda b, h, i: (b, i, h)),
        out_shape=jax.ShapeDtypeStruct((bsz, s, width), BF16),
        scratch_shapes=[pltpu.VMEM((LANES, tq), F32), pltpu.VMEM((1, tq), F32)],
        compiler_params=_params("arbitrary", "arbitrary", "arbitrary"),
        name="stick_breaking",
    )(q, k4, vt, jnp.asarray(upper, BF16))


def _dilated_kernel(q_ref, k_ref, v_ref, o_ref, acc_ref, m_ref, l_ref, *, tile):
    ti = pl.program_id(2)
    lane = lax.broadcasted_iota(jnp.int32, (1, LANES), 1)
    q_lanes = [(lane & 32) == 0, (lane & 32) != 0]
    v_lanes = [lane < HEAD_DIM, lane >= HEAD_DIM]

    for bi, (_, d) in enumerate(B_PATTERNS):
        ts = min(DIL_SUB, tile // d)
        band = ts + B_BACK
        nsub = tile // d // ts

        def body(it, carry, d=d, ts=ts, band=band, nsub=nsub, first=(bi == 0)):
            r = it // nsub
            sub = it % nsub
            row0 = r + d * sub * ts
            m0 = (ti * tile) // d + sub * ts
            ks = jnp.maximum(m0 - B_BACK, 0)
            if d == 1:
                q_idx, k_idx = pl.ds(row0, ts), pl.ds(ks, band)
            else:
                q_idx, k_idx = pl.ds(row0, ts, stride=d), pl.ds(r + d * ks, band, stride=d)
            qs = q_ref[0, q_idx, :]
            kb = k_ref[0, k_idx, :].astype(BF16)
            vb = v_ref[0, k_idx, :]
            dist = ((m0 + lax.broadcasted_iota(jnp.int32, (ts, band), 0))
                    - (ks + lax.broadcasted_iota(jnp.int32, (ts, band), 1)))
            valid = jnp.where(dist >= 0, dist, B_BACK + 1) <= B_BACK
            o_loc, m_loc, l_loc = None, None, None
            for h in range(2):
                s = _nt(jnp.where(q_lanes[h], qs, 0.0).astype(BF16), kb)
                s = jnp.where(valid, s, NEG)
                mh = jnp.max(s, axis=1, keepdims=True)
                p = jnp.exp2(s - mh)
                lh = jnp.sum(p, axis=1, keepdims=True)
                oh = _mm(p.astype(BF16), jnp.where(v_lanes[h], vb, 0.0).astype(BF16))
                if h == 0:
                    o_loc, m_loc, l_loc = oh, mh, lh
                else:
                    o_loc = o_loc + oh
                    m_loc = jnp.where(v_lanes[0], m_loc, mh)
                    l_loc = jnp.where(v_lanes[0], l_loc, lh)
            if first:
                acc_ref[q_idx, :] = o_loc
                m_ref[q_idx, :] = m_loc
                l_ref[q_idx, :] = l_loc
            else:
                m_old = m_ref[q_idx, :]
                m_new = jnp.maximum(m_old, m_loc)
                a_old = jnp.exp2(m_old - m_new)
                a_loc = jnp.exp2(m_loc - m_new)
                acc_ref[q_idx, :] = acc_ref[q_idx, :] * a_old + o_loc * a_loc
                l_ref[q_idx, :] = l_ref[q_idx, :] * a_old + l_loc * a_loc
                m_ref[q_idx, :] = m_new
            return carry

        lax.fori_loop(0, d * nsub, body, 0)

    o_ref[0] = (acc_ref[...] / l_ref[...]).astype(o_ref.dtype)


def _dilated_attention(q, k, v):
    bsz, s, width = q.shape
    tile = min(DIL_TILE, s)
    d_max = B_PATTERNS[-1][1]
    assert s % tile == 0 and (min(DIL_SUB, tile // d_max) + B_BACK) * d_max <= s
    kv_spec = pl.BlockSpec((1, s, LANES), lambda b, h, i: (b, 0, h), pipeline_mode=pl.Buffered(1))
    return pl.pallas_call(
        functools.partial(_dilated_kernel, tile=tile),
        grid=(bsz, width // LANES, s // tile),
        in_specs=[pl.BlockSpec((1, tile, LANES), lambda b, h, i: (b, i, h)), kv_spec, kv_spec],
        out_specs=pl.BlockSpec((1, tile, LANES), lambda b, h, i: (b, i, h)),
        out_shape=jax.ShapeDtypeStruct((bsz, s, width), BF16),
        scratch_shapes=[pltpu.VMEM((tile, LANES), F32)] * 3,
        compiler_params=_params("arbitrary", "arbitrary", "arbitrary"),
        name="dilated",
    )(q, k, v)


def _post_kernel(*refs, tm, final):
    if final:
        (x_ref, o1_ref, o2_ref, wo1_ref, wo2_ref, g_ref, wg_ref, wv_ref, cw_ref, cb_ref, wd_ref, fg_ref,
         out_ref, gs_ref) = refs
    else:
        (x_ref, o1_ref, o2_ref, wo1_ref, wo2_ref, g_ref, wg_ref, wv_ref, cw_ref, cb_ref, wd_ref,
         out_ref, gs_ref) = refs

    @pl.when(pl.program_id(1) == 0)
    def _():
        gs_ref[0:8, :] = jnp.zeros((8, D_FF), F32)

    x1 = x_ref[0] + _mm(o1_ref[0], wo1_ref[...]) + _mm(o2_ref[0], wo2_ref[...])
    h = _rms(x1, g_ref[...]).astype(BF16)
    out_ref[0] = x1
    for c in range(D_FF // FF_CHUNK):
        sl = slice(c * FF_CHUNK, (c + 1) * FF_CHUNK)
        gate = _mm(h, wg_ref[:, sl])
        val = _mm(h, wv_ref[:, sl])
        gs_ref[8:8 + tm, sl] = gate
        conv = (cw_ref[0:1, sl] * gs_ref[6:6 + tm, sl] + cw_ref[1:2, sl] * gs_ref[7:7 + tm, sl]
                + cw_ref[2:3, sl] * gate + cb_ref[:, sl])
        act = conv * jax.nn.sigmoid(conv) * val
        out_ref[0] += _mm(act.astype(BF16), wd_ref[sl, :])
        gs_ref[0:8, sl] = gs_ref[tm:tm + 8, sl]
    if final:
        out_ref[0] = _rms(out_ref[0], fg_ref[...])


def _post(x, o1, o2, w_out, ffn_g, w_up, conv_w, conv_b, w_down, final_g):
    bsz, s, _ = x.shape
    tm = min(TOK_TILE, s)
    final = final_g is not None
    tok = lambda w: pl.BlockSpec((1, tm, w), lambda b, i: (b, i, 0))
    in_specs = [tok(D_MODEL), tok(HALF), tok(HALF),
                _const_spec((HALF, D_MODEL)), _const_spec((HALF, D_MODEL)), _const_spec((1, D_MODEL)),
                _const_spec((D_MODEL, D_FF)), _const_spec((D_MODEL, D_FF)),
                _const_spec((3, D_FF)), _const_spec((1, D_FF)), _const_spec((D_FF, D_MODEL))]
    args = [x, o1, o2, w_out[:HALF].astype(BF16), w_out[HALF:].astype(BF16), ffn_g.reshape(1, D_MODEL),
            w_up[:, :D_FF].astype(BF16), w_up[:, D_FF:].astype(BF16), conv_w, conv_b.reshape(1, D_FF),
            w_down.astype(BF16)]
    if final:
        in_specs.append(_const_spec((1, D_MODEL)))
        args.append(final_g.reshape(1, D_MODEL))
    return pl.pallas_call(
        functools.partial(_post_kernel, tm=tm, final=final),
        grid=(bsz, s // tm),
        in_specs=in_specs,
        out_specs=tok(D_MODEL),
        out_shape=jax.ShapeDtypeStruct(x.shape, F32),
        scratch_shapes=[pltpu.VMEM((tm + 8, D_FF), F32)],
        compiler_params=_params("arbitrary", "arbitrary"),
        name="outproj_mlp",
    )(*args)


def _pair_perm():
    j = np.arange(LANES)
    return (j % 64) // 32 * 64 + j // 64 * 32 + j % 32


def _perm512():
    return np.concatenate([b * LANES + _pair_perm() for b in range(HALF // LANES)])


def _rope_tables(seq):
    def tab(dim):
        inv_freq = ROPE_THETA ** (-jnp.arange(0, dim, 2, dtype=F32) / dim)
        ang = jnp.arange(seq, dtype=F32)[:, None] * inv_freq[None, :]
        return jnp.cos(ang), jnp.sin(ang)
    c, s = tab(HEAD_DIM)
    cos_ab = jnp.concatenate([c, c, c, c], axis=1)
    sin_ab = jnp.concatenate([-s, -s, s, s], axis=1)
    c, s = tab(C_ROPE_DIM)
    one = jnp.ones((seq, 32), F32)
    pad1 = jnp.ones((seq, 16), F32)
    zero = jnp.zeros((seq, 32), F32)
    pad0 = jnp.zeros((seq, 16), F32)
    cos_c = jnp.concatenate([one, c, pad1, one, c, pad1], axis=1)
    sin_c = jnp.concatenate([zero, -s, pad0, zero, s, pad0], axis=1)
    return cos_ab, sin_ab, cos_c, sin_c


def _mla_cols(per_head_src):
    dst, src = [], []
    for hd in range(C_HEADS):
        base = hd * LANES
        nope, rope = per_head_src(hd)
        for n in range(C_NOPE_DIM):
            dst.append(base + (n if n < 32 else 64 + n - 32))
            src.append(nope + n)
        if rope is not None:
            for t in range(C_ROPE_DIM):
                dst.append(base + (32 + t if t < 16 else 96 + t - 16))
                src.append(rope + t)
    return np.array(dst), np.array(src)


def _proj_even(x, g, w_in, cos, sin):
    bsz, s, _ = x.shape
    tm = min(TOK_TILE, s)
    perm = _perm512()
    aq, ak, av, bq, bk, bv = (w_in[:, i * HALF:(i + 1) * HALF] for i in range(6))
    wm = jnp.concatenate([aq[:, perm], ak[:, perm], bq[:, perm], bk[:, perm], bv], axis=1).astype(BF16)
    wvt = av.T.astype(BF16)
    tok = lambda w: pl.BlockSpec((1, tm, w), lambda b, i: (b, i, 0))
    vt_spec = pl.BlockSpec((1, tm // KV_TILE, HALF, KV_TILE), lambda b, i: (b, i, 0, 0))
    tab = pl.BlockSpec((tm, LANES), lambda b, i: (i, 0))
    sd = jax.ShapeDtypeStruct
    return pl.pallas_call(
        _proj_even_kernel,
        grid=(bsz, s // tm),
        in_specs=[tok(D_MODEL), _const_spec((1, D_MODEL)), _const_spec((D_MODEL, 5 * HALF)),
                  _const_spec((HALF, D_MODEL)), tab, tab],
        out_specs=[tok(HALF), tok(HALF), vt_spec, tok(HALF), tok(HALF), tok(HALF)],
        out_shape=[sd((bsz, s, HALF), BF16), sd((bsz, s, HALF), BF16),
                   sd((bsz, s // KV_TILE, HALF, KV_TILE), BF16),
                   sd((bsz, s, HALF), F32), sd((bsz, s, HALF), F32), sd((bsz, s, HALF), F32)],
        compiler_params=_params("arbitrary", "arbitrary"),
        name="proj_even",
    )(x, g.reshape(1, D_MODEL), wm, wvt, cos, sin)


def _proj_odd(x, g, w_in, q_norm_g, w_uq, kv_norm_g, w_ukv, cos, sin):
    bsz, s, _ = x.shape
    tm = min(TOK_TILE, s)
    o_kv, o_kr, o_d = C_Q_RANK, C_Q_RANK + C_KV_RANK, C_Q_RANK + C_KV_RANK + C_ROPE_DIM
    kr_dst = np.concatenate([32 + np.arange(16), 96 + np.arange(16)])
    w_kr = jnp.zeros((D_MODEL, LANES), F32).at[:, kr_dst].set(w_in[:, o_kr:o_kr + C_ROPE_DIM])
    wm = jnp.concatenate([w_in[:, :o_kr], w_kr, w_in[:, o_d:o_d + 2 * HALF]], axis=1).astype(BF16)
    wdvt = w_in[:, o_d + 2 * HALF:o_d + 3 * HALF].T.astype(BF16)
    qk_dim = C_NOPE_DIM + C_ROPE_DIM
    dst, src = _mla_cols(lambda hd: (hd * qk_dim, hd * qk_dim + C_NOPE_DIM))
    wuq = jnp.zeros((C_Q_RANK, HALF), F32).at[:, dst].set(w_uq[:, src]).astype(BF16)
    kv_dim = C_NOPE_DIM + C_V_DIM
    dst, src = _mla_cols(lambda hd: (hd * kv_dim, None))
    wukvk = jnp.zeros((C_KV_RANK, HALF), F32).at[:, dst].set(w_ukv[:, src]).astype(BF16)
    v_cols = np.concatenate([hd * kv_dim + C_NOPE_DIM + np.arange(C_V_DIM) for hd in range(C_HEADS)])
    wukvvt = w_ukv[:, v_cols].T.astype(BF16)
    tok = lambda w: pl.BlockSpec((1, tm, w), lambda b, i: (b, i, 0))
    vt_spec = lambda tkv: pl.BlockSpec((1, tm // tkv, HALF, tkv), lambda b, i: (b, i, 0, 0))
    tab = pl.BlockSpec((tm, LANES), lambda b, i: (i, 0))
    sd = jax.ShapeDtypeStruct
    return pl.pallas_call(
        _proj_odd_kernel,
        grid=(bsz, s // tm),
        in_specs=[tok(D_MODEL), _const_spec((1, D_MODEL)), _const_spec(wm.shape),
                  _const_spec((HALF, D_MODEL)), _const_spec((1, C_Q_RANK)), _const_spec((C_Q_RANK, HALF)),
                  _const_spec((1, C_KV_RANK)), _const_spec((C_KV_RANK, HALF)), _const_spec((HALF, C_KV_RANK)),
                  tab, tab],
        out_specs=[tok(HALF), tok(HALF), vt_spec(KV_TILE), tok(HALF), tok(HALF), vt_spec(SB_KV_TILE)],
        out_shape=[sd((bsz, s, HALF), BF16), sd((bsz, s, HALF), BF16),
                   sd((bsz, s // KV_TILE, HALF, KV_TILE), BF16),
                   sd((bsz, s, HALF), BF16), sd((bsz, s, HALF), BF16),
                   sd((bsz, s // SB_KV_TILE, HALF, SB_KV_TILE), BF16)],
        compiler_params=_params("arbitrary", "arbitrary"),
        name="proj_odd",
    )(x, g.reshape(1, D_MODEL), wm, wdvt, q_norm_g.reshape(1, C_Q_RANK), wuq,
      kv_norm_g.reshape(1, C_KV_RANK), wukvk, wukvvt, cos, sin)


def kernel(x, mix_norm_g, ffn_norm_g, final_norm_g, ab_w_in, a_lambda_q1, a_lambda_k1, a_lambda_q2,
           a_lambda_k2, a_subln_g, ab_w_out, cd_w_in, c_q_norm_g, c_w_uq, c_kv_norm_g, c_w_ukv, cd_w_out,
           ffn_w_up, ffn_conv_w, ffn_conv_b, ffn_w_down):
    depth = mix_norm_g.shape[0]
    seq = x.shape[1]
    cos_ab, sin_ab, cos_c, sin_c = _rope_tables(seq)
    for layer in range(depth):
        j = layer // 2
        if layer % 2 == 0:
            lam_init = 0.8 - 0.6 * math.exp(-0.3 * layer)
            qa, ka, vat, qb, kb, vb = _proj_even(x, mix_norm_g[layer], ab_w_in[j], cos_ab, sin_ab)
            o1 = _flash_attention(qa, ka, vat, n_maps=2, lam_init=lam_init,
                                  lam_params=(a_lambda_q1[j], a_lambda_k1[j], a_lambda_q2[j], a_lambda_k2[j]),
                                  subln_g=a_subln_g[j])
            o2 = _dilated_attention(qb, kb, vb)
            w_out = ab_w_out[j]
        else:
            qc, kc, vct, qd, kd, vdt = _proj_odd(x, mix_norm_g[layer], cd_w_in[j], c_q_norm_g[j], c_w_uq[j],
                                                 c_kv_norm_g[j], c_w_ukv[j], cos_c, sin_c)
            o1 = _flash_attention(qc, kc, vct, n_maps=1)
            o2 = _stick_attention(qd, kd, vdt)
            w_out = cd_w_out[j]
        x = _post(x, o1, o2, w_out, ffn_norm_g[layer], ffn_w_up[layer], ffn_conv_w[layer], ffn_conv_b[layer],
                  ffn_w_down[layer], final_norm_g if layer == depth - 1 else None)
    return x
```

```python
import functools
import math

import numpy as np
import jax
import jax.numpy as jnp
from jax import lax
from jax.experimental import pallas as pl
from jax.experimental.pallas import tpu as pltpu

F32 = jnp.float32
BF16 = jnp.bfloat16

D_MODEL = 1024
HEAD_DIM = 64
ROPE_THETA = 10000.0
NORM_EPS = 1e-6
A_HEADS = 4
B_PATTERNS = ((128, 1), (512, 4), (2048, 16))
B_BACK = 128
C_HEADS = 4
C_NOPE_DIM = 64
C_ROPE_DIM = 32
C_V_DIM = 128
C_Q_RANK = 384
C_KV_RANK = 256
D_HEADS = 4
D_HEAD_DIM = 128
D_FF = 2816
HALF = 512

LOG2E = 1.4426950408889634
NEG = -1e30
LANES = 128
VMEM_LIMIT = 56 * 1024 * 1024

TOK_TILE = 512
Q_TILE = 1024
KV_TILE = 512
SB_Q_TILE = 512
SB_KV_TILE = 256
SB_DEAD = -160.0
FF_CHUNK = 2816
DIL_TILE = 2048
DIL_SUB = 256


def _nt(a, b):
    return lax.dot_general(a, b, (((1,), (1,)), ((), ())), preferred_element_type=F32)


def _mm(a, b):
    return jnp.dot(a, b, preferred_element_type=F32)


def _rms(x, g):
    return x * lax.rsqrt(jnp.mean(x * x, axis=-1, keepdims=True) + NORM_EPS) * g


def _params(*sem):
    return pltpu.CompilerParams(dimension_semantics=sem, vmem_limit_bytes=VMEM_LIMIT)


def _const_spec(shape):
    n = len(shape)
    return pl.BlockSpec(shape, lambda *_: (0,) * n, pipeline_mode=pl.Buffered(1))


def _rope(blk, cos, sin):
    return blk * cos + pltpu.roll(blk, 64, 1) * sin


def _store_vt(vt_ref, vt, tkv):
    for t in range(vt.shape[1] // tkv):
        vt_ref[0, t] = vt[:, t * tkv:(t + 1) * tkv].astype(BF16)


def _proj_even_kernel(x_ref, g_ref, wm_ref, wvt_ref, cos_ref, sin_ref,
                      qa_ref, ka_ref, vat_ref, qb_ref, kb_ref, vb_ref):
    h = _rms(x_ref[0], g_ref[...]).astype(BF16)
    p = _mm(h, wm_ref[...])
    cos, sin = cos_ref[...], sin_ref[...]
    qscale = HEAD_DIM ** -0.5 * LOG2E
    for b in range(HALF // LANES):
        c = slice(b * LANES, (b + 1) * LANES)
        qa_ref[0, :, c] = (_rope(p[:, c], cos, sin) * qscale).astype(BF16)
        ka_ref[0, :, c] = _rope(p[:, HALF + b * LANES:HALF + (b + 1) * LANES], cos, sin).astype(BF16)
        qb_ref[0, :, c] = _rope(p[:, 2 * HALF + b * LANES:2 * HALF + (b + 1) * LANES], cos, sin) * qscale
        kb_ref[0, :, c] = _rope(p[:, 3 * HALF + b * LANES:3 * HALF + (b + 1) * LANES], cos, sin)
    vb_ref[0] = p[:, 4 * HALF:5 * HALF]
    _store_vt(vat_ref, _nt(wvt_ref[...], h), KV_TILE)


def _proj_odd_kernel(x_ref, g_ref, wm_ref, wdvt_ref, qg_ref, wuq_ref, kvg_ref, wukvk_ref, wukvvt_ref,
                     cos_ref, sin_ref, qc_ref, kc_ref, vct_ref, qd_ref, kd_ref, vdt_ref):
    h = _rms(x_ref[0], g_ref[...]).astype(BF16)
    p = _mm(h, wm_ref[...])
    cos, sin = cos_ref[...], sin_ref[...]
    cqn = _rms(p[:, :C_Q_RANK], qg_ref[...]).astype(BF16)
    q = _mm(cqn, wuq_ref[...])
    o_kv = C_Q_RANK
    ckvn = _rms(p[:, o_kv:o_kv + C_KV_RANK], kvg_ref[...]).astype(BF16)
    kn = _mm(ckvn, wukvk_ref[...])
    _store_vt(vct_ref, _nt(wukvvt_ref[...], ckvn), KV_TILE)
    o_kr = o_kv + C_KV_RANK
    kr = _rope(p[:, o_kr:o_kr + LANES], cos, sin)
    cscale = (C_NOPE_DIM + C_ROPE_DIM) ** -0.5 * LOG2E
    for b in range(C_HEADS):
        c = slice(b * LANES, (b + 1) * LANES)
        qc_ref[0, :, c] = (_rope(q[:, c], cos, sin) * cscale).astype(BF16)
        kc_ref[0, :, c] = (kn[:, c] + kr).astype(BF16)
    o_dq = o_kr + LANES
    qd_ref[0] = (p[:, o_dq:o_dq + HALF] * (D_HEAD_DIM ** -0.5 * LOG2E)).astype(BF16)
    kd_ref[0] = p[:, o_dq + HALF:o_dq + 2 * HALF].astype(BF16)
    _store_vt(vdt_ref, _nt(wdvt_ref[...], h), SB_KV_TILE)


def _flash_kernel(*refs, n_maps, tq, tk, lam_init):
    if n_maps == 2:
        (q_ref, k_ref, vt_ref, lq1_ref, lk1_ref, lq2_ref, lk2_ref, g_ref, o_ref,
         s_ref, mt_ref, at_ref, m_ref, l_ref, acc_ref) = refs
    else:
        q_ref, k_ref, vt_ref, o_ref, s_ref, mt_ref, at_ref, m_ref, l_ref, acc_ref = refs
    i = pl.program_id(2)
    nblk = tq // tk
    q = q_ref[0]
    if n_maps == 2:
        lane = lax.broadcasted_iota(jnp.int32, (1, LANES), 1)
        zero = jnp.zeros_like(q)
        qm = [jnp.where((lane & 32) == 0, q, zero), jnp.where((lane & 32) != 0, q, zero)]
    else:
        qm = [q]
    chains = [(c, g) for c in range(n_maps) for g in range(nblk)]
    qs = [qm[c][g * tk:(g + 1) * tk] for (c, g) in chains]
    m_ref[...] = jnp.full(m_ref.shape, NEG, F32)
    l_ref[...] = jnp.zeros(l_ref.shape, F32)
    acc_ref[...] = jnp.zeros(acc_ref.shape, F32)
    causal = (lax.broadcasted_iota(jnp.int32, (tk, tk), 0) <= lax.broadcasted_iota(jnp.int32, (tk, tk), 1))

    def stage1(j, slot, active):
        k = k_ref[0, j]
        for ch, masked in active.items():
            s = _nt(k, qs[ch])
            if masked:
                s = jnp.where(causal, s, NEG)
            m_old = m_ref[ch]
            m_new = jnp.maximum(m_old, jnp.max(s, axis=0, keepdims=True))
            s_ref[slot, ch] = s
            mt_ref[slot, ch] = m_new
            at_ref[slot, ch] = jnp.exp2(m_old - m_new)
            m_ref[ch] = m_new

    def stage2(j, slot, active):
        vt = vt_ref[0, j]
        for ch in active:
            p = jnp.exp2(s_ref[slot, ch] - mt_ref[slot, ch])
            alpha = at_ref[slot, ch]
            l_ref[ch] = alpha * l_ref[ch] + jnp.sum(p, axis=0, keepdims=True)
            acc_ref[ch] = alpha * acc_ref[ch] + _mm(vt, p.astype(BF16))

    def run(seq):
        slot = 0
        for idx, (j, active) in enumerate(seq):
            if idx + 1 < len(seq):
                stage1(seq[idx + 1][0], 1 - slot, seq[idx + 1][1])
            stage2(j, slot, active)
            slot = 1 - slot

    full = {ch: False for ch in range(len(chains))}
    diag = [{ch: g == d for ch, (_, g) in enumerate(chains) if g >= d} for d in range(nblk)]
    n_full = i * nblk

    @pl.when(i > 0)
    def _():
        stage1(0, 0, full)

        def pair(u, carry):
            stage1(2 * u + 1, 1, full)
            stage2(2 * u, 0, full)
            stage1(2 * u + 2, 0, full)
            stage2(2 * u + 1, 1, full)
            return carry

        n_pair = (n_full - 1) // 2
        lax.fori_loop(0, n_pair, pair, 0)
        done = 2 * n_pair
        for rem in (1, 2):
            @pl.when(n_full - done == rem)
            def _(rem=rem):
                run([(done + t, full) for t in range(rem)] + [(n_full + d, diag[d]) for d in range(nblk)])

    @pl.when(i == 0)
    def _():
        stage1(0, 0, diag[0])
        run([(d, diag[d]) for d in range(nblk)])

    outs = [jnp.concatenate([acc_ref[c * nblk + g] / l_ref[c * nblk + g] for g in range(nblk)], axis=1)
            for c in range(n_maps)]
    if n_maps == 2:
        lam = (jnp.exp(jnp.sum(lq1_ref[...] * lk1_ref[...], axis=-1, keepdims=True))
               - jnp.exp(jnp.sum(lq2_ref[...] * lk2_ref[...], axis=-1, keepdims=True)) + lam_init)
        o = outs[0] - lam * outs[1]
        o = o * lax.rsqrt(jnp.mean(o * o, axis=0, keepdims=True) + NORM_EPS) * g_ref[...]
        o = o * (1.0 - lam_init)
    else:
        o = outs[0]
    o_ref[0] = o.T.astype(o_ref.dtype)


def _flash_attention(q, k, vt, *, n_maps, lam_init=0.0, lam_params=None, subln_g=None):
    bsz, s, width = q.shape
    heads = width // LANES
    tq = min(Q_TILE, s)
    tk = vt.shape[-1]
    assert tq % tk == 0 and s % tq == 0
    n_chain = n_maps * tq // tk
    k4 = k.reshape(bsz, s // tk, tk, width)
    in_specs = [
        pl.BlockSpec((1, tq, LANES), lambda b, h, i: (b, i, h)),
        pl.BlockSpec((1, s // tk, tk, LANES), lambda b, h, i: (b, 0, 0, h)),
        pl.BlockSpec((1, s // tk, LANES, tk), lambda b, h, i: (b, 0, h, 0)),
    ]
    args = [q, k4, vt]
    if n_maps == 2:
        in_specs += [pl.BlockSpec((1, HEAD_DIM), lambda b, h, i: (0, 0))] * 4
        in_specs += [pl.BlockSpec((LANES, 1), lambda b, h, i: (0, 0))]
        args += [p.reshape(1, HEAD_DIM) for p in lam_params] + [subln_g.reshape(LANES, 1)]
    return pl.pallas_call(
        functools.partial(_flash_kernel, n_maps=n_maps, tq=tq, tk=tk, lam_init=lam_init),
        grid=(bsz, heads, s // tq),
        in_specs=in_specs,
        out_specs=pl.BlockSpec((1, tq, LANES), lambda b, h, i: (b, i, h)),
        out_shape=jax.ShapeDtypeStruct((bsz, s, width), BF16),
        scratch_shapes=[pltpu.VMEM((2, n_chain, tk, tk), F32),
                        pltpu.VMEM((2, n_chain, 1, tk), F32),
                        pltpu.VMEM((2, n_chain, 1, tk), F32),
                        pltpu.VMEM((n_chain, 1, tk), F32),
                        pltpu.VMEM((n_chain, 1, tk), F32),
                        pltpu.VMEM((n_chain, LANES, tk), F32)],
        compiler_params=_params("arbitrary", "arbitrary", "arbitrary"),
        name="flash_diff" if n_maps == 2 else "flash_mla",
    )(*args)


def _stick_kernel(q_ref, k_ref, vt_ref, u_ref, o_ref, acc_ref, r_ref, *, tq, tk):
    i = pl.program_id(2)
    q = q_ref[0]
    u = u_ref[...]
    acc_ref[...] = jnp.zeros(acc_ref.shape, F32)
    r_ref[...] = jnp.zeros(r_ref.shape, F32)

    def tile(j, mask):
        z = _nt(k_ref[0, j], q)
        lk = -(jnp.maximum(z, 0.0) + jnp.log2(1.0 + jnp.exp2(-jnp.abs(z))))
        if mask is not None:
            lk = jnp.where(mask, lk, 0.0)
        la = z + lk + _mm(u, lk.astype(BF16)) + r_ref[...]
        if mask is not None:
            la = jnp.where(mask, la, NEG)
        acc_ref[...] += _mm(vt_ref[0, j], jnp.exp2(la).astype(BF16))
        r_ref[...] += jnp.sum(lk, axis=0, keepdims=True)

    nsub = tq // tk
    n_full = i * nsub
    for jj in reversed(range(nsub)):
        row = lax.broadcasted_iota(jnp.int32, (tk, tq), 0) + jj * tk
        col = lax.broadcasted_iota(jnp.int32, (tk, tq), 1)
        tile(n_full + jj, row < col)

    def live():
        return jnp.max(r_ref[...]) > SB_DEAD

    def cond(carry):
        t, alive = carry
        return jnp.logical_and(t < n_full, alive)

    def body(carry):
        t, _ = carry
        tile(n_full - 1 - t, None)
        return t + 1, live()

    lax.while_loop(cond, body, (jnp.int32(0), live()))
    o_ref[0] = acc_ref[...].T.astype(o_ref.dtype)


def _stick_attention(q, k, vt):
    bsz, s, width = q.shape
    heads = width // LANES
    tq = min(SB_Q_TILE, s)
    tk = vt.shape[-1]
    assert tq % tk == 0 and s % tq == 0
    k4 = k.reshape(bsz, s // tk, tk, width)
    upper = (np.arange(tk)[None, :] > np.arange(tk)[:, None]).astype(np.float32)
    return pl.pallas_call(
        functools.partial(_stick_kernel, tq=tq, tk=tk),
        grid=(bsz, heads, s // tq),
        in_specs=[
            pl.BlockSpec((1, tq, LANES), lambda b, h, i: (b, i, h)),
            pl.BlockSpec((1, s // tk, tk, LANES), lambda b, h, i: (b, 0, 0, h)),
            pl.BlockSpec((1, s // tk, LANES, tk), lambda b, h, i: (b, 0, h, 0)),
            pl.BlockSpec((tk, tk), lambda b, h, i: (0, 0)),
        ],
        out_specs=pl.BlockSpec((1, tq, LANES), lambda b, h, i: (b, i, h)),
        out_shape=jax.ShapeDtypeStruct((bsz, s, width), BF16),
        scratch_shapes=[pltpu.VMEM((LANES, tq), F32), pltpu.VMEM((1, tq), F32)],
        compiler_params=_params("arbitrary", "arbitrary", "arbitrary"),
        name="stick_breaking",
    )(q, k4, vt, jnp.asarray(upper, BF16))


def _dilated_kernel(q_ref, k_ref, v_ref, bias_a_ref, bias_b_ref, o_ref, s_ref, acc_ref, m_ref, l_ref, *, tile):
    ti = pl.program_id(2)
    lane = lax.broadcasted_iota(jnp.int32, (1, LANES), 1)
    q_lanes = [(lane & 32) == 0, (lane & 32) != 0]
    head0 = lane < HEAD_DIM

    for bi, (_, d) in enumerate(B_PATTERNS):
        ts = min(DIL_SUB, tile // d)
        band = ts + B_BACK
        nsub = tile // d // ts
        n_step = d * nsub
        bias_ref = bias_a_ref if ts == DIL_SUB else bias_b_ref
        first = bi == 0

        def index(it, d=d, ts=ts, band=band, nsub=nsub):
            r = it // nsub
            sub = it % nsub
            row0 = r + d * sub * ts
            m0 = (ti * tile) // d + sub * ts
            ks = jnp.maximum(m0 - B_BACK, 0)
            if d == 1:
                return pl.ds(row0, ts), pl.ds(ks, band), m0
            return pl.ds(row0, ts, stride=d), pl.ds(r + d * ks, band, stride=d), m0

        def stage1(it, slot, ts=ts, band=band, index=index):
            q_idx, k_idx, _ = index(it)
            qs = q_ref[0, q_idx, :]
            kb = k_ref[0, k_idx, :].astype(BF16)
            for h in range(2):
                s_ref[slot, h, 0:ts, 0:band] = _nt(jnp.where(q_lanes[h], qs, 0.0).astype(BF16), kb)

        def stage2(it, slot, ts=ts, band=band, index=index, bias_ref=bias_ref, first=first):
            q_idx, k_idx, m0 = index(it)
            vb = v_ref[0, k_idx, :].astype(BF16)
            bias = bias_ref[(m0 < B_BACK).astype(jnp.int32)]
            stats = []
            for h in range(2):
                s = s_ref[slot, h, 0:ts, 0:band] + bias
                mh = jnp.max(s, axis=1, keepdims=True)
                p = jnp.exp2(s - mh)
                stats.append((_mm(p.astype(BF16), vb), mh, jnp.sum(p, axis=1, keepdims=True)))
            o_loc, m_loc, l_loc = (jnp.where(head0, a, b) for a, b in zip(*stats))
            if first:
                acc_ref[q_idx, :] = o_loc
                m_ref[q_idx, :] = m_loc
                l_ref[q_idx, :] = l_loc
            else:
                m_old = m_ref[q_idx, :]
                m_new = jnp.maximum(m_old, m_loc)
                a_old = jnp.exp2(m_old - m_new)
                a_loc = jnp.exp2(m_loc - m_new)
                acc_ref[q_idx, :] = acc_ref[q_idx, :] * a_old + o_loc * a_loc
                l_ref[q_idx, :] = l_ref[q_idx, :] * a_old + l_loc * a_loc
                m_ref[q_idx, :] = m_new

        assert n_step % 2 == 0
        stage1(0, 0)

        def pair(u, carry, stage1=stage1, stage2=stage2):
            stage1(2 * u + 1, 1)
            stage2(2 * u, 0)
            stage1(2 * u + 2, 0)
            stage2(2 * u + 1, 1)
            return carry

        lax.fori_loop(0, n_step // 2 - 1, pair, 0)
        stage1(n_step - 1, 1)
        stage2(n_step - 2, 0)
        stage2(n_step - 1, 1)

    o_ref[0] = (acc_ref[...] / l_ref[...]).astype(o_ref.dtype)


def _window_bias(ts):
    qi = np.arange(ts)[:, None]
    kj = np.arange(ts + B_BACK)[None, :]
    out = []
    for offset in (B_BACK, 0):
        dist = qi + offset - kj
        out.append(np.where((dist >= 0) & (dist <= B_BACK), 0.0, NEG))
    return jnp.asarray(np.stack(out), F32)


def _dilated_attention(q, k, v):
    bsz, s, width = q.shape
    tile = min(DIL_TILE, s)
    d_max = B_PATTERNS[-1][1]
    ts_min = min(DIL_SUB, tile // d_max)
    assert s % tile == 0 and tile % (d_max * B_BACK) == 0 and (ts_min + B_BACK) * d_max <= s
    kv_spec = pl.BlockSpec((1, s, LANES), lambda b, h, i: (b, 0, h), pipeline_mode=pl.Buffered(1))
    bias_a, bias_b = _window_bias(DIL_SUB), _window_bias(ts_min)
    return pl.pallas_call(
        functools.partial(_dilated_kernel, tile=tile),
        grid=(bsz, width // LANES, s // tile),
        in_specs=[pl.BlockSpec((1, tile, LANES), lambda b, h, i: (b, i, h)), kv_spec, kv_spec,
                  _const_spec(bias_a.shape), _const_spec(bias_b.shape)],
        out_specs=pl.BlockSpec((1, tile, LANES), lambda b, h, i: (b, i, h)),
        out_shape=jax.ShapeDtypeStruct((bsz, s, width), BF16),
        scratch_shapes=[pltpu.VMEM((2, 2, DIL_SUB, DIL_SUB + B_BACK), F32)] + [pltpu.VMEM((tile, LANES), F32)] * 3,
        compiler_params=_params("arbitrary", "arbitrary", "arbitrary"),
        name="dilated",
    )(q, k, v, bias_a, bias_b)


def _post_kernel(*refs, tm, final):
    if final:
        (x_ref, o1_ref, o2_ref, wo1_ref, wo2_ref, g_ref, wg_ref, wv_ref, cw_ref, cb_ref, wd_ref, fg_ref,
         out_ref, gs_ref) = refs
    else:
        (x_ref, o1_ref, o2_ref, wo1_ref, wo2_ref, g_ref, wg_ref, wv_ref, cw_ref, cb_ref, wd_ref,
         out_ref, gs_ref) = refs

    @pl.when(pl.program_id(1) == 0)
    def _():
        gs_ref[0:8, :] = jnp.zeros((8, D_FF), F32)

    x1 = x_ref[0] + _mm(o1_ref[0], wo1_ref[...]) + _mm(o2_ref[0], wo2_ref[...])
    h = _rms(x1, g_ref[...]).astype(BF16)
    out_ref[0] = x1
    for c in range(D_FF // FF_CHUNK):
        sl = slice(c * FF_CHUNK, (c + 1) * FF_CHUNK)
        gate = _mm(h, wg_ref[:, sl])
        val = _mm(h, wv_ref[:, sl])
        gs_ref[8:8 + tm, sl] = gate
        conv = (cw_ref[0:1, sl] * gs_ref[6:6 + tm, sl] + cw_ref[1:2, sl] * gs_ref[7:7 + tm, sl]
                + cw_ref[2:3, sl] * gate + cb_ref[:, sl])
        act = conv * jax.nn.sigmoid(conv) * val
        out_ref[0] += _mm(act.astype(BF16), wd_ref[sl, :])
        gs_ref[0:8, sl] = gs_ref[tm:tm + 8, sl]
    if final:
        out_ref[0] = _rms(out_ref[0], fg_ref[...])


def _post(x, o1, o2, w_out, ffn_g, w_up, conv_w, conv_b, w_down, final_g):
    bsz, s, _ = x.shape
    tm = min(TOK_TILE, s)
    final = final_g is not None
    tok = lambda w: pl.BlockSpec((1, tm, w), lambda b, i: (b, i, 0))
    in_specs = [tok(D_MODEL), tok(HALF), tok(HALF),
                _const_spec((HALF, D_MODEL)), _const_spec((HALF, D_MODEL)), _const_spec((1, D_MODEL)),
                _const_spec((D_MODEL, D_FF)), _const_spec((D_MODEL, D_FF)),
                _const_spec((3, D_FF)), _const_spec((1, D_FF)), _const_spec((D_FF, D_MODEL))]
    args = [x, o1, o2, w_out[:HALF].astype(BF16), w_out[HALF:].astype(BF16), ffn_g.reshape(1, D_MODEL),
            w_up[:, :D_FF].astype(BF16), w_up[:, D_FF:].astype(BF16), conv_w, conv_b.reshape(1, D_FF),
            w_down.astype(BF16)]
    if final:
        in_specs.append(_const_spec((1, D_MODEL)))
        args.append(final_g.reshape(1, D_MODEL))
    return pl.pallas_call(
        functools.partial(_post_kernel, tm=tm, final=final),
        grid=(bsz, s // tm),
        in_specs=in_specs,
        out_specs=tok(D_MODEL),
        out_shape=jax.ShapeDtypeStruct(x.shape, F32),
        scratch_shapes=[pltpu.VMEM((tm + 8, D_FF), F32)],
        compiler_params=_params("arbitrary", "arbitrary"),
        name="outproj_mlp",
    )(*args)


def _pair_perm():
    j = np.arange(LANES)
    return (j % 64) // 32 * 64 + j // 64 * 32 + j % 32


def _perm512():
    return np.concatenate([b * LANES + _pair_perm() for b in range(HALF // LANES)])


def _rope_tables(seq):
    def tab(dim):
        inv_freq = ROPE_THETA ** (-jnp.arange(0, dim, 2, dtype=F32) / dim)
        ang = jnp.arange(seq, dtype=F32)[:, None] * inv_freq[None, :]
        return jnp.cos(ang), jnp.sin(ang)
    c, s = tab(HEAD_DIM)
    cos_ab = jnp.concatenate([c, c, c, c], axis=1)
    sin_ab = jnp.concatenate([-s, -s, s, s], axis=1)
    c, s = tab(C_ROPE_DIM)
    one = jnp.ones((seq, 32), F32)
    pad1 = jnp.ones((seq, 16), F32)
    zero = jnp.zeros((seq, 32), F32)
    pad0 = jnp.zeros((seq, 16), F32)
    cos_c = jnp.concatenate([one, c, pad1, one, c, pad1], axis=1)
    sin_c = jnp.concatenate([zero, -s, pad0, zero, s, pad0], axis=1)
    return cos_ab, sin_ab, cos_c, sin_c


def _mla_cols(per_head_src):
    dst, src = [], []
    for hd in range(C_HEADS):
        base = hd * LANES
        nope, rope = per_head_src(hd)
        for n in range(C_NOPE_DIM):
            dst.append(base + (n if n < 32 else 64 + n - 32))
            src.append(nope + n)
        if rope is not None:
            for t in range(C_ROPE_DIM):
                dst.append(base + (32 + t if t < 16 else 96 + t - 16))
                src.append(rope + t)
    return np.array(dst), np.array(src)


def _proj_even(x, g, w_in, cos, sin):
    bsz, s, _ = x.shape
    tm = min(TOK_TILE, s)
    perm = _perm512()
    aq, ak, av, bq, bk, bv = (w_in[:, i * HALF:(i + 1) * HALF] for i in range(6))
    wm = jnp.concatenate([aq[:, perm], ak[:, perm], bq[:, perm], bk[:, perm], bv], axis=1).astype(BF16)
    wvt = av.T.astype(BF16)
    tok = lambda w: pl.BlockSpec((1, tm, w), lambda b, i: (b, i, 0))
    vt_spec = pl.BlockSpec((1, tm // KV_TILE, HALF, KV_TILE), lambda b, i: (b, i, 0, 0))
    tab = pl.BlockSpec((tm, LANES), lambda b, i: (i, 0))
    sd = jax.ShapeDtypeStruct
    return pl.pallas_call(
        _proj_even_kernel,
        grid=(bsz, s // tm),
        in_specs=[tok(D_MODEL), _const_spec((1, D_MODEL)), _const_spec((D_MODEL, 5 * HALF)),
                  _const_spec((HALF, D_MODEL)), tab, tab],
        out_specs=[tok(HALF), tok(HALF), vt_spec, tok(HALF), tok(HALF), tok(HALF)],
        out_shape=[sd((bsz, s, HALF), BF16), sd((bsz, s, HALF), BF16),
                   sd((bsz, s // KV_TILE, HALF, KV_TILE), BF16),
                   sd((bsz, s, HALF), F32), sd((bsz, s, HALF), F32), sd((bsz, s, HALF), F32)],
        compiler_params=_params("arbitrary", "arbitrary"),
        name="proj_even",
    )(x, g.reshape(1, D_MODEL), wm, wvt, cos, sin)


def _proj_odd(x, g, w_in, q_norm_g, w_uq, kv_norm_g, w_ukv, cos, sin):
    bsz, s, _ = x.shape
    tm = min(TOK_TILE, s)
    o_kv, o_kr, o_d = C_Q_RANK, C_Q_RANK + C_KV_RANK, C_Q_RANK + C_KV_RANK + C_ROPE_DIM
    kr_dst = np.concatenate([32 + np.arange(16), 96 + np.arange(16)])
    w_kr = jnp.zeros((D_MODEL, LANES), F32).at[:, kr_dst].set(w_in[:, o_kr:o_kr + C_ROPE_DIM])
    wm = jnp.concatenate([w_in[:, :o_kr], w_kr, w_in[:, o_d:o_d + 2 * HALF]], axis=1).astype(BF16)
    wdvt = w_in[:, o_d + 2 * HALF:o_d + 3 * HALF].T.astype(BF16)
    qk_dim = C_NOPE_DIM + C_ROPE_DIM
    dst, src = _mla_cols(lambda hd: (hd * qk_dim, hd * qk_dim + C_NOPE_DIM))
    wuq = jnp.zeros((C_Q_RANK, HALF), F32).at[:, dst].set(w_uq[:, src]).astype(BF16)
    kv_dim = C_NOPE_DIM + C_V_DIM
    dst, src = _mla_cols(lambda hd: (hd * kv_dim, None))
    wukvk = jnp.zeros((C_KV_RANK, HALF), F32).at[:, dst].set(w_ukv[:, src]).astype(BF16)
    v_cols = np.concatenate([hd * kv_dim + C_NOPE_DIM + np.arange(C_V_DIM) for hd in range(C_HEADS)])
    wukvvt = w_ukv[:, v_cols].T.astype(BF16)
    tok = lambda w: pl.BlockSpec((1, tm, w), lambda b, i: (b, i, 0))
    vt_spec = lambda tkv: pl.BlockSpec((1, tm // tkv, HALF, tkv), lambda b, i: (b, i, 0, 0))
    tab = pl.BlockSpec((tm, LANES), lambda b, i: (i, 0))
    sd = jax.ShapeDtypeStruct
    return pl.pallas_call(
        _proj_odd_kernel,
        grid=(bsz, s // tm),
        in_specs=[tok(D_MODEL), _const_spec((1, D_MODEL)), _const_spec(wm.shape),
                  _const_spec((HALF, D_MODEL)), _const_spec((1, C_Q_RANK)), _const_spec((C_Q_RANK, HALF)),
                  _const_spec((1, C_KV_RANK)), _const_spec((C_KV_RANK, HALF)), _const_spec((HALF, C_KV_RANK)),
                  tab, tab],
        out_specs=[tok(HALF), tok(HALF), vt_spec(KV_TILE), tok(HALF), tok(HALF), vt_spec(SB_KV_TILE)],
        out_shape=[sd((bsz, s, HALF), BF16), sd((bsz, s, HALF), BF16),
                   sd((bsz, s // KV_TILE, HALF, KV_TILE), BF16),
                   sd((bsz, s, HALF), BF16), sd((bsz, s, HALF), BF16),
                   sd((bsz, s // SB_KV_TILE, HALF, SB_KV_TILE), BF16)],
        compiler_params=_params("arbitrary", "arbitrary"),
        name="proj_odd",
    )(x, g.reshape(1, D_MODEL), wm, wdvt, q_norm_g.reshape(1, C_Q_RANK), wuq,
      kv_norm_g.reshape(1, C_KV_RANK), wukvk, wukvvt, cos, sin)


def kernel(x, mix_norm_g, ffn_norm_g, final_norm_g, ab_w_in, a_lambda_q1, a_lambda_k1, a_lambda_q2,
           a_lambda_k2, a_subln_g, ab_w_out, cd_w_in, c_q_norm_g, c_w_uq, c_kv_norm_g, c_w_ukv, cd_w_out,
           ffn_w_up, ffn_conv_w, ffn_conv_b, ffn_w_down):
    depth = mix_norm_g.shape[0]
    seq = x.shape[1]
    cos_ab, sin_ab, cos_c, sin_c = _rope_tables(seq)
    for layer in range(depth):
        j = layer // 2
        if layer % 2 == 0:
            lam_init = 0.8 - 0.6 * math.exp(-0.3 * layer)
            qa, ka, vat, qb, kb, vb = _proj_even(x, mix_norm_g[layer], ab_w_in[j], cos_ab, sin_ab)
            o1 = _flash_attention(qa, ka, vat, n_maps=2, lam_init=lam_init,
                                  lam_params=(a_lambda_q1[j], a_lambda_k1[j], a_lambda_q2[j], a_lambda_k2[j]),
                                  subln_g=a_subln_g[j])
            o2 = _dilated_attention(qb, kb, vb)
            w_out = ab_w_out[j]
        else:
            qc, kc, vct, qd, kd, vdt = _proj_odd(x, mix_norm_g[layer], cd_w_in[j], c_q_norm_g[j], c_w_uq[j],
                                                 c_kv_norm_g[j], c_w_ukv[j], cos_c, sin_c)
            o1 = _flash_attention(qc, kc, vct, n_maps=1)
            o2 = _stick_attention(qd, kd, vdt)
            w_out = cd_w_out[j]
        x = _post(x, o1, o2, w_out, ffn_norm_g[layer], ffn_w_up[layer], ffn_conv_w[layer], ffn_conv_b[layer],
                  ffn_w_down[layer], final_norm_g if layer == depth - 1 else None)
    return x
```

```python
import functools
import math

import numpy as np
import jax
import jax.numpy as jnp
from jax import lax
from jax.experimental import pallas as pl
from jax.experimental.pallas import tpu as pltpu

F32 = jnp.float32
BF16 = jnp.bfloat16

D_MODEL = 1024
HEAD_DIM = 64
ROPE_THETA = 10000.0
NORM_EPS = 1e-6
A_HEADS = 4
B_PATTERNS = ((128, 1), (512, 4), (2048, 16))
B_BACK = 128
C_HEADS = 4
C_NOPE_DIM = 64
C_ROPE_DIM = 32
C_V_DIM = 128
C_Q_RANK = 384
C_KV_RANK = 256
D_HEADS = 4
D_HEAD_DIM = 128
D_FF = 2816
HALF = 512

LOG2E = 1.4426950408889634
NEG = -1e30
LANES = 128
VMEM_LIMIT = 56 * 1024 * 1024

TOK_TILE = 512
Q_TILE = 1024
KV_TILE = 512
PAIR_UNROLL = 3
SB_Q_TILE = 512
SB_KV_TILE = 256
SB_DEAD = -160.0
FF_CHUNK = 2816
DIL_TILE = 2048
DIL_SUB = 256


def _nt(a, b):
    return lax.dot_general(a, b, (((1,), (1,)), ((), ())), preferred_element_type=F32)


def _mm(a, b):
    return jnp.dot(a, b, preferred_element_type=F32)


def _rms(x, g):
    return x * lax.rsqrt(jnp.mean(x * x, axis=-1, keepdims=True) + NORM_EPS) * g


def _params(*sem):
    return pltpu.CompilerParams(dimension_semantics=sem, vmem_limit_bytes=VMEM_LIMIT)


def _const_spec(shape):
    n = len(shape)
    return pl.BlockSpec(shape, lambda *_: (0,) * n, pipeline_mode=pl.Buffered(1))


def _rope(blk, cos, sin):
    return blk * cos + pltpu.roll(blk, 64, 1) * sin


def _store_vt(vt_ref, vt, tkv):
    for t in range(vt.shape[1] // tkv):
        vt_ref[0, t] = vt[:, t * tkv:(t + 1) * tkv].astype(BF16)


def _proj_even_kernel(x_ref, g_ref, wm_ref, wvt_ref, cos_ref, sin_ref,
                      qa_ref, ka_ref, vat_ref, qb_ref, kb_ref, vb_ref):
    h = _rms(x_ref[0], g_ref[...]).astype(BF16)
    p = _mm(h, wm_ref[...])
    cos, sin = cos_ref[...], sin_ref[...]
    qscale = HEAD_DIM ** -0.5 * LOG2E
    for b in range(HALF // LANES):
        c = slice(b * LANES, (b + 1) * LANES)
        qa_ref[0, :, c] = (_rope(p[:, c], cos, sin) * qscale).astype(BF16)
        ka_ref[0, :, c] = _rope(p[:, HALF + b * LANES:HALF + (b + 1) * LANES], cos, sin).astype(BF16)
        qb_ref[0, :, c] = _rope(p[:, 2 * HALF + b * LANES:2 * HALF + (b + 1) * LANES], cos, sin) * qscale
        kb_ref[0, :, c] = _rope(p[:, 3 * HALF + b * LANES:3 * HALF + (b + 1) * LANES], cos, sin)
    vb_ref[0] = p[:, 4 * HALF:5 * HALF]
    _store_vt(vat_ref, _nt(wvt_ref[...], h), KV_TILE)


def _proj_odd_kernel(x_ref, g_ref, wm_ref, wdvt_ref, qg_ref, wuq_ref, kvg_ref, wukvk_ref, wukvvt_ref,
                     cos_ref, sin_ref, qc_ref, kc_ref, vct_ref, qd_ref, kd_ref, vdt_ref):
    h = _rms(x_ref[0], g_ref[...]).astype(BF16)
    p = _mm(h, wm_ref[...])
    cos, sin = cos_ref[...], sin_ref[...]
    cqn = _rms(p[:, :C_Q_RANK], qg_ref[...]).astype(BF16)
    q = _mm(cqn, wuq_ref[...])
    o_kv = C_Q_RANK
    ckvn = _rms(p[:, o_kv:o_kv + C_KV_RANK], kvg_ref[...]).astype(BF16)
    kn = _mm(ckvn, wukvk_ref[...])
    _store_vt(vct_ref, _nt(wukvvt_ref[...], ckvn), KV_TILE)
    o_kr = o_kv + C_KV_RANK
    kr = _rope(p[:, o_kr:o_kr + LANES], cos, sin)
    cscale = (C_NOPE_DIM + C_ROPE_DIM) ** -0.5 * LOG2E
    for b in range(C_HEADS):
        c = slice(b * LANES, (b + 1) * LANES)
        qc_ref[0, :, c] = (_rope(q[:, c], cos, sin) * cscale).astype(BF16)
        kc_ref[0, :, c] = (kn[:, c] + kr).astype(BF16)
    o_dq = o_kr + LANES
    qd_ref[0] = (p[:, o_dq:o_dq + HALF] * (D_HEAD_DIM ** -0.5 * LOG2E)).astype(BF16)
    kd_ref[0] = p[:, o_dq + HALF:o_dq + 2 * HALF].astype(BF16)
    _store_vt(vdt_ref, _nt(wdvt_ref[...], h), SB_KV_TILE)


def _flash_kernel(*refs, n_maps, tq, tk, lam_init):
    if n_maps == 2:
        (q_ref, k_ref, vt_ref, lq1_ref, lk1_ref, lq2_ref, lk2_ref, g_ref, o_ref,
         s_ref, mt_ref, at_ref, m_ref, l_ref, acc_ref) = refs
    else:
        q_ref, k_ref, vt_ref, o_ref, s_ref, mt_ref, at_ref, m_ref, l_ref, acc_ref = refs
    i = pl.program_id(2)
    nblk = tq // tk
    q = q_ref[0]
    if n_maps == 2:
        lane = lax.broadcasted_iota(jnp.int32, (1, LANES), 1)
        zero = jnp.zeros_like(q)
        qm = [jnp.where((lane & 32) == 0, q, zero), jnp.where((lane & 32) != 0, q, zero)]
    else:
        qm = [q]
    chains = [(c, g) for c in range(n_maps) for g in range(nblk)]
    qs = [qm[c][g * tk:(g + 1) * tk] for (c, g) in chains]
    m_ref[...] = jnp.full(m_ref.shape, NEG, F32)
    l_ref[...] = jnp.zeros(l_ref.shape, F32)
    acc_ref[...] = jnp.zeros(acc_ref.shape, F32)
    causal = (lax.broadcasted_iota(jnp.int32, (tk, tk), 0) <= lax.broadcasted_iota(jnp.int32, (tk, tk), 1))

    def stage1(j, slot, active):
        k = k_ref[0, j]
        for ch, masked in active.items():
            s = _nt(k, qs[ch])
            if masked:
                s = jnp.where(causal, s, NEG)
            m_old = m_ref[ch]
            m_new = jnp.maximum(m_old, jnp.max(s, axis=0, keepdims=True))
            s_ref[slot, ch] = s
            mt_ref[slot, ch] = m_new
            at_ref[slot, ch] = jnp.exp2(m_old - m_new)
            m_ref[ch] = m_new

    def stage2(j, slot, active):
        vt = vt_ref[0, j]
        for ch in active:
            p = jnp.exp2(s_ref[slot, ch] - mt_ref[slot, ch])
            alpha = at_ref[slot, ch]
            l_ref[ch] = alpha * l_ref[ch] + jnp.sum(p, axis=0, keepdims=True)
            acc_ref[ch] = alpha * acc_ref[ch] + _mm(vt, p.astype(BF16))

    def run(seq):
        slot = 0
        for idx, (j, active) in enumerate(seq):
            if idx + 1 < len(seq):
                stage1(seq[idx + 1][0], 1 - slot, seq[idx + 1][1])
            stage2(j, slot, active)
            slot = 1 - slot

    full = {ch: False for ch in range(len(chains))}
    diag = [{ch: g == d for ch, (_, g) in enumerate(chains) if g >= d} for d in range(nblk)]
    n_full = i * nblk

    @pl.when(i > 0)
    def _():
        stage1(0, 0, full)

        def pair(u, carry):
            stage1(2 * u + 1, 1, full)
            stage2(2 * u, 0, full)
            stage1(2 * u + 2, 0, full)
            stage2(2 * u + 1, 1, full)
            return carry

        n_pair = (n_full - 1) // 2
        unroll = PAIR_UNROLL if n_maps == 1 else 1

        def pairs(w, carry):
            for t in range(unroll):
                pair(w * unroll + t, carry)
            return carry

        lax.fori_loop(0, n_pair // unroll, pairs, 0)
        lax.fori_loop(n_pair // unroll * unroll, n_pair, pair, 0)
        done = 2 * n_pair
        for rem in (1, 2):
            @pl.when(n_full - done == rem)
            def _(rem=rem):
                run([(done + t, full) for t in range(rem)] + [(n_full + d, diag[d]) for d in range(nblk)])

    @pl.when(i == 0)
    def _():
        stage1(0, 0, diag[0])
        run([(d, diag[d]) for d in range(nblk)])

    outs = [jnp.concatenate([acc_ref[c * nblk + g] / l_ref[c * nblk + g] for g in range(nblk)], axis=1)
            for c in range(n_maps)]
    if n_maps == 2:
        lam = (jnp.exp(jnp.sum(lq1_ref[...] * lk1_ref[...], axis=-1, keepdims=True))
               - jnp.exp(jnp.sum(lq2_ref[...] * lk2_ref[...], axis=-1, keepdims=True)) + lam_init)
        o = outs[0] - lam * outs[1]
        o = o * lax.rsqrt(jnp.mean(o * o, axis=0, keepdims=True) + NORM_EPS) * g_ref[...]
        o = o * (1.0 - lam_init)
    else:
        o = outs[0]
    o_ref[0] = o.T.astype(o_ref.dtype)


def _flash_attention(q, k, vt, *, n_maps, lam_init=0.0, lam_params=None, subln_g=None):
    bsz, s, width = q.shape
    heads = width // LANES
    tq = min(Q_TILE, s)
    tk = vt.shape[-1]
    assert tq % tk == 0 and s % tq == 0
    n_chain = n_maps * tq // tk
    k4 = k.reshape(bsz, s // tk, tk, width)
    in_specs = [
        pl.BlockSpec((1, tq, LANES), lambda b, h, i: (b, i, h)),
        pl.BlockSpec((1, s // tk, tk, LANES), lambda b, h, i: (b, 0, 0, h)),
        pl.BlockSpec((1, s // tk, LANES, tk), lambda b, h, i: (b, 0, h, 0)),
    ]
    args = [q, k4, vt]
    if n_maps == 2:
        in_specs += [pl.BlockSpec((1, HEAD_DIM), lambda b, h, i: (0, 0))] * 4
        in_specs += [pl.BlockSpec((LANES, 1), lambda b, h, i: (0, 0))]
        args += [p.reshape(1, HEAD_DIM) for p in lam_params] + [subln_g.reshape(LANES, 1)]
    return pl.pallas_call(
        functools.partial(_flash_kernel, n_maps=n_maps, tq=tq, tk=tk, lam_init=lam_init),
        grid=(bsz, heads, s // tq),
        in_specs=in_specs,
        out_specs=pl.BlockSpec((1, tq, LANES), lambda b, h, i: (b, i, h)),
        out_shape=jax.ShapeDtypeStruct((bsz, s, width), BF16),
        scratch_shapes=[pltpu.VMEM((2, n_chain, tk, tk), F32),
                        pltpu.VMEM((2, n_chain, 1, tk), F32),
                        pltpu.VMEM((2, n_chain, 1, tk), F32),
                        pltpu.VMEM((n_chain, 1, tk), F32),
                        pltpu.VMEM((n_chain, 1, tk), F32),
                        pltpu.VMEM((n_chain, LANES, tk), F32)],
        compiler_params=_params("arbitrary", "arbitrary", "arbitrary"),
        name="flash_diff" if n_maps == 2 else "flash_mla",
    )(*args)


def _stick_kernel(q_ref, k_ref, vt_ref, u_ref, o_ref, acc_ref, r_ref, *, tq, tk):
    i = pl.program_id(2)
    q = q_ref[0]
    u = u_ref[...]
    acc_ref[...] = jnp.zeros(acc_ref.shape, F32)
    r_ref[...] = jnp.zeros(r_ref.shape, F32)

    def group(tiles):
        zs, lks = [], []
        for j, mask, c0 in tiles:
            z = _nt(k_ref[0, j], q[c0:])
            lk = -(jnp.maximum(z, 0.0) + jnp.log2(1.0 + jnp.exp2(-jnp.abs(z))))
            if mask is not None:
                lk = jnp.where(mask[:, c0:], lk, 0.0)
            zs.append(z)
            lks.append(lk)
        ws = [_mm(u, lk.astype(BF16)) for lk in lks]
        for (j, mask, c0), z, lk, w in zip(tiles, zs, lks, ws):
            la = z + lk + w + r_ref[:, c0:]
            if mask is not None:
                la = jnp.where(mask[:, c0:], la, NEG)
            acc_ref[:, c0:] += _mm(vt_ref[0, j], jnp.exp2(la).astype(BF16))
            r_ref[:, c0:] += jnp.sum(lk, axis=0, keepdims=True)

    nsub = tq // tk
    n_full = i * nsub
    diag = []
    for jj in reversed(range(nsub)):
        row = lax.broadcasted_iota(jnp.int32, (tk, tq), 0) + jj * tk
        col = lax.broadcasted_iota(jnp.int32, (tk, tq), 1)
        diag.append((n_full + jj, row < col, jj * tk))

    @pl.when(i > 0)
    def _():
        group(diag + [(n_full - 1, None, 0)])

    @pl.when(i == 0)
    def _():
        group(diag)

    def live():
        return jnp.max(r_ref[...]) > SB_DEAD

    def cond(carry):
        t, alive = carry
        return jnp.logical_and(t < n_full, alive)

    def body(carry):
        t, _ = carry
        group([(n_full - 1 - t, None, 0)])
        return t + 1, live()

    lax.while_loop(cond, body, (jnp.int32(1), live()))
    o_ref[0] = acc_ref[...].T.astype(o_ref.dtype)


def _stick_attention(q, k, vt):
    bsz, s, width = q.shape
    heads = width // LANES
    tq = min(SB_Q_TILE, s)
    tk = vt.shape[-1]
    assert tq % tk == 0 and s % tq == 0
    k4 = k.reshape(bsz, s // tk, tk, width)
    upper = (np.arange(tk)[None, :] > np.arange(tk)[:, None]).astype(np.float32)
    return pl.pallas_call(
        functools.partial(_stick_kernel, tq=tq, tk=tk),
        grid=(bsz, heads, s // tq),
        in_specs=[
            pl.BlockSpec((1, tq, LANES), lambda b, h, i: (b, i, h)),
            pl.BlockSpec((1, s // tk, tk, LANES), lambda b, h, i: (b, 0, 0, h)),
            pl.BlockSpec((1, s // tk, LANES, tk), lambda b, h, i: (b, 0, h, 0)),
            pl.BlockSpec((tk, tk), lambda b, h, i: (0, 0)),
        ],
        out_specs=pl.BlockSpec((1, tq, LANES), lambda b, h, i: (b, i, h)),
        out_shape=jax.ShapeDtypeStruct((bsz, s, width), BF16),
        scratch_shapes=[pltpu.VMEM((LANES, tq), F32), pltpu.VMEM((1, tq), F32)],
        compiler_params=_params("arbitrary", "arbitrary", "arbitrary"),
        name="stick_breaking",
    )(q, k4, vt, jnp.asarray(upper, BF16))


def _dilated_kernel(q_ref, k_ref, v_ref, bias_a_ref, bias_b_ref, o_ref, s_ref, acc_ref, m_ref, l_ref, *, tile):
    ti = pl.program_id(2)
    lane = lax.broadcasted_iota(jnp.int32, (1, LANES), 1)
    q_lanes = [(lane & 32) == 0, (lane & 32) != 0]
    head0 = lane < HEAD_DIM

    for bi, (_, d) in enumerate(B_PATTERNS):
        ts = min(DIL_SUB, tile // d)
        band = ts + B_BACK
        nsub = tile // d // ts
        n_step = d * nsub
        bias_ref = bias_a_ref if ts == DIL_SUB else bias_b_ref
        first = bi == 0

        def index(it, d=d, ts=ts, band=band, nsub=nsub):
            r = it // nsub
            sub = it % nsub
            row0 = r + d * sub * ts
            m0 = (ti * tile) // d + sub * ts
            ks = jnp.maximum(m0 - B_BACK, 0)
            if d == 1:
                return pl.ds(row0, ts), pl.ds(ks, band), m0
            return pl.ds(row0, ts, stride=d), pl.ds(r + d * ks, band, stride=d), m0

        def stage1(it, slot, ts=ts, band=band, index=index):
            q_idx, k_idx, _ = index(it)
            qs = q_ref[0, q_idx, :]
            kb = k_ref[0, k_idx, :].astype(BF16)
            for h in range(2):
                s_ref[slot, h, 0:ts, 0:band] = _nt(jnp.where(q_lanes[h], qs, 0.0).astype(BF16), kb)

        def stage2(it, slot, ts=ts, band=band, index=index, bias_ref=bias_ref, first=first):
            q_idx, k_idx, m0 = index(it)
            vb = v_ref[0, k_idx, :].astype(BF16)
            bias = bias_ref[(m0 < B_BACK).astype(jnp.int32)]
            stats = []
            for h in range(2):
                s = s_ref[slot, h, 0:ts, 0:band] + bias
                mh = jnp.max(s, axis=1, keepdims=True)
                p = jnp.exp2(s - mh)
                stats.append((_mm(p.astype(BF16), vb), mh, jnp.sum(p, axis=1, keepdims=True)))
            o_loc, m_loc, l_loc = (jnp.where(head0, a, b) for a, b in zip(*stats))
            if first:
                acc_ref[q_idx, :] = o_loc
                m_ref[q_idx, :] = m_loc
                l_ref[q_idx, :] = l_loc
            else:
                m_old = m_ref[q_idx, :]
                m_new = jnp.maximum(m_old, m_loc)
                a_old = jnp.exp2(m_old - m_new)
                a_loc = jnp.exp2(m_loc - m_new)
                acc_ref[q_idx, :] = acc_ref[q_idx, :] * a_old + o_loc * a_loc
                l_ref[q_idx, :] = l_ref[q_idx, :] * a_old + l_loc * a_loc
                m_ref[q_idx, :] = m_new

        assert n_step % 2 == 0
        stage1(0, 0)

        def pair(u, carry, stage1=stage1, stage2=stage2):
            stage1(2 * u + 1, 1)
            stage2(2 * u, 0)
            stage1(2 * u + 2, 0)
            stage2(2 * u + 1, 1)
            return carry

        lax.fori_loop(0, n_step // 2 - 1, pair, 0)
        stage1(n_step - 1, 1)
        stage2(n_step - 2, 0)
        stage2(n_step - 1, 1)

    o_ref[0] = (acc_ref[...] / l_ref[...]).astype(o_ref.dtype)


def _window_bias(ts):
    qi = np.arange(ts)[:, None]
    kj = np.arange(ts + B_BACK)[None, :]
    out = []
    for offset in (B_BACK, 0):
        dist = qi + offset - kj
        out.append(np.where((dist >= 0) & (dist <= B_BACK), 0.0, NEG))
    return jnp.asarray(np.stack(out), F32)


def _dilated_attention(q, k, v):
    bsz, s, width = q.shape
    tile = min(DIL_TILE, s)
    d_max = B_PATTERNS[-1][1]
    ts_min = min(DIL_SUB, tile // d_max)
    assert s % tile == 0 and tile % (d_max * B_BACK) == 0 and (ts_min + B_BACK) * d_max <= s
    kv_spec = pl.BlockSpec((1, s, LANES), lambda b, h, i: (b, 0, h), pipeline_mode=pl.Buffered(1))
    bias_a, bias_b = _window_bias(DIL_SUB), _window_bias(ts_min)
    return pl.pallas_call(
        functools.partial(_dilated_kernel, tile=tile),
        grid=(bsz, width // LANES, s // tile),
        in_specs=[pl.BlockSpec((1, tile, LANES), lambda b, h, i: (b, i, h)), kv_spec, kv_spec,
                  _const_spec(bias_a.shape), _const_spec(bias_b.shape)],
        out_specs=pl.BlockSpec((1, tile, LANES), lambda b, h, i: (b, i, h)),
        out_shape=jax.ShapeDtypeStruct((bsz, s, width), BF16),
        scratch_shapes=[pltpu.VMEM((2, 2, DIL_SUB, DIL_SUB + B_BACK), F32)] + [pltpu.VMEM((tile, LANES), F32)] * 3,
        compiler_params=_params("arbitrary", "arbitrary", "arbitrary"),
        name="dilated",
    )(q, k, v, bias_a, bias_b)


def _post_kernel(*refs, tm, final):
    if final:
        (x_ref, o1_ref, o2_ref, wo1_ref, wo2_ref, g_ref, wg_ref, wv_ref, cw_ref, cb_ref, wd_ref, fg_ref,
         out_ref, gs_ref) = refs
    else:
        (x_ref, o1_ref, o2_ref, wo1_ref, wo2_ref, g_ref, wg_ref, wv_ref, cw_ref, cb_ref, wd_ref,
         out_ref, gs_ref) = refs

    @pl.when(pl.program_id(1) == 0)
    def _():
        gs_ref[0:8, :] = jnp.zeros((8, D_FF), F32)

    x1 = x_ref[0] + _mm(o1_ref[0], wo1_ref[...]) + _mm(o2_ref[0], wo2_ref[...])
    h = _rms(x1, g_ref[...]).astype(BF16)
    out_ref[0] = x1
    for c in range(D_FF // FF_CHUNK):
        sl = slice(c * FF_CHUNK, (c + 1) * FF_CHUNK)
        gate = _mm(h, wg_ref[:, sl])
        val = _mm(h, wv_ref[:, sl])
        gs_ref[8:8 + tm, sl] = gate
        conv = (cw_ref[0:1, sl] * gs_ref[6:6 + tm, sl] + cw_ref[1:2, sl] * gs_ref[7:7 + tm, sl]
                + cw_ref[2:3, sl] * gate + cb_ref[:, sl])
        act = conv * jax.nn.sigmoid(conv) * val
        out_ref[0] += _mm(act.astype(BF16), wd_ref[sl, :])
        gs_ref[0:8, sl] = gs_ref[tm:tm + 8, sl]
    if final:
        out_ref[0] = _rms(out_ref[0], fg_ref[...])


def _post(x, o1, o2, w_out, ffn_g, w_up, conv_w, conv_b, w_down, final_g):
    bsz, s, _ = x.shape
    tm = min(TOK_TILE, s)
    final = final_g is not None
    tok = lambda w: pl.BlockSpec((1, tm, w), lambda b, i: (b, i, 0))
    in_specs = [tok(D_MODEL), tok(HALF), tok(HALF),
                _const_spec((HALF, D_MODEL)), _const_spec((HALF, D_MODEL)), _const_spec((1, D_MODEL)),
                _const_spec((D_MODEL, D_FF)), _const_spec((D_MODEL, D_FF)),
                _const_spec((3, D_FF)), _const_spec((1, D_FF)), _const_spec((D_FF, D_MODEL))]
    args = [x, o1, o2, w_out[:HALF].astype(BF16), w_out[HALF:].astype(BF16), ffn_g.reshape(1, D_MODEL),
            w_up[:, :D_FF].astype(BF16), w_up[:, D_FF:].astype(BF16), conv_w, conv_b.reshape(1, D_FF),
            w_down.astype(BF16)]
    if final:
        in_specs.append(_const_spec((1, D_MODEL)))
        args.append(final_g.reshape(1, D_MODEL))
    return pl.pallas_call(
        functools.partial(_post_kernel, tm=tm, final=final),
        grid=(bsz, s // tm),
        in_specs=in_specs,
        out_specs=tok(D_MODEL),
        out_shape=jax.ShapeDtypeStruct(x.shape, F32),
        scratch_shapes=[pltpu.VMEM((tm + 8, D_FF), F32)],
        compiler_params=_params("arbitrary", "arbitrary"),
        name="outproj_mlp",
    )(*args)


def _pair_perm():
    j = np.arange(LANES)
    return (j % 64) // 32 * 64 + j // 64 * 32 + j % 32


def _perm512():
    return np.concatenate([b * LANES + _pair_perm() for b in range(HALF // LANES)])


def _rope_tables(seq):
    def tab(dim):
        inv_freq = ROPE_THETA ** (-jnp.arange(0, dim, 2, dtype=F32) / dim)
        ang = jnp.arange(seq, dtype=F32)[:, None] * inv_freq[None, :]
        return jnp.cos(ang), jnp.sin(ang)
    c, s = tab(HEAD_DIM)
    cos_ab = jnp.concatenate([c, c, c, c], axis=1)
    sin_ab = jnp.concatenate([-s, -s, s, s], axis=1)
    c, s = tab(C_ROPE_DIM)
    one = jnp.ones((seq, 32), F32)
    pad1 = jnp.ones((seq, 16), F32)
    zero = jnp.zeros((seq, 32), F32)
    pad0 = jnp.zeros((seq, 16), F32)
    cos_c = jnp.concatenate([one, c, pad1, one, c, pad1], axis=1)
    sin_c = jnp.concatenate([zero, -s, pad0, zero, s, pad0], axis=1)
    return cos_ab, sin_ab, cos_c, sin_c


def _mla_cols(per_head_src):
    dst, src = [], []
    for hd in range(C_HEADS):
        base = hd * LANES
        nope, rope = per_head_src(hd)
        for n in range(C_NOPE_DIM):
            dst.append(base + (n if n < 32 else 64 + n - 32))
            src.append(nope + n)
        if rope is not None:
            for t in range(C_ROPE_DIM):
                dst.append(base + (32 + t if t < 16 else 96 + t - 16))
                src.append(rope + t)
    return np.array(dst), np.array(src)


def _proj_even(x, g, w_in, cos, sin):
    bsz, s, _ = x.shape
    tm = min(TOK_TILE, s)
    perm = _perm512()
    aq, ak, av, bq, bk, bv = (w_in[:, i * HALF:(i + 1) * HALF] for i in range(6))
    wm = jnp.concatenate([aq[:, perm], ak[:, perm], bq[:, perm], bk[:, perm], bv], axis=1).astype(BF16)
    wvt = av.T.astype(BF16)
    tok = lambda w: pl.BlockSpec((1, tm, w), lambda b, i: (b, i, 0))
    vt_spec = pl.BlockSpec((1, tm // KV_TILE, HALF, KV_TILE), lambda b, i: (b, i, 0, 0))
    tab = pl.BlockSpec((tm, LANES), lambda b, i: (i, 0))
    sd = jax.ShapeDtypeStruct
    return pl.pallas_call(
        _proj_even_kernel,
        grid=(bsz, s // tm),
        in_specs=[tok(D_MODEL), _const_spec((1, D_MODEL)), _const_spec((D_MODEL, 5 * HALF)),
                  _const_spec((HALF, D_MODEL)), tab, tab],
        out_specs=[tok(HALF), tok(HALF), vt_spec, tok(HALF), tok(HALF), tok(HALF)],
        out_shape=[sd((bsz, s, HALF), BF16), sd((bsz, s, HALF), BF16),
                   sd((bsz, s // KV_TILE, HALF, KV_TILE), BF16),
                   sd((bsz, s, HALF), F32), sd((bsz, s, HALF), F32), sd((bsz, s, HALF), F32)],
        compiler_params=_params("arbitrary", "arbitrary"),
        name="proj_even",
    )(x, g.reshape(1, D_MODEL), wm, wvt, cos, sin)


def _proj_odd(x, g, w_in, q_norm_g, w_uq, kv_norm_g, w_ukv, cos, sin):
    bsz, s, _ = x.shape
    tm = min(TOK_TILE, s)
    o_kv, o_kr, o_d = C_Q_RANK, C_Q_RANK + C_KV_RANK, C_Q_RANK + C_KV_RANK + C_ROPE_DIM
    kr_dst = np.concatenate([32 + np.arange(16), 96 + np.arange(16)])
    w_kr = jnp.zeros((D_MODEL, LANES), F32).at[:, kr_dst].set(w_in[:, o_kr:o_kr + C_ROPE_DIM])
    wm = jnp.concatenate([w_in[:, :o_kr], w_kr, w_in[:, o_d:o_d + 2 * HALF]], axis=1).astype(BF16)
    wdvt = w_in[:, o_d + 2 * HALF:o_d + 3 * HALF].T.astype(BF16)
    qk_dim = C_NOPE_DIM + C_ROPE_DIM
    dst, src = _mla_cols(lambda hd: (hd * qk_dim, hd * qk_dim + C_NOPE_DIM))
    wuq = jnp.zeros((C_Q_RANK, HALF), F32).at[:, dst].set(w_uq[:, src]).astype(BF16)
    kv_dim = C_NOPE_DIM + C_V_DIM
    dst, src = _mla_cols(lambda hd: (hd * kv_dim, None))
    wukvk = jnp.zeros((C_KV_RANK, HALF), F32).at[:, dst].set(w_ukv[:, src]).astype(BF16)
    v_cols = np.concatenate([hd * kv_dim + C_NOPE_DIM + np.arange(C_V_DIM) for hd in range(C_HEADS)])
    wukvvt = w_ukv[:, v_cols].T.astype(BF16)
    tok = lambda w: pl.BlockSpec((1, tm, w), lambda b, i: (b, i, 0))
    vt_spec = lambda tkv: pl.BlockSpec((1, tm // tkv, HALF, tkv), lambda b, i: (b, i, 0, 0))
    tab = pl.BlockSpec((tm, LANES), lambda b, i: (i, 0))
    sd = jax.ShapeDtypeStruct
    return pl.pallas_call(
        _proj_odd_kernel,
        grid=(bsz, s // tm),
        in_specs=[tok(D_MODEL), _const_spec((1, D_MODEL)), _const_spec(wm.shape),
                  _const_spec((HALF, D_MODEL)), _const_spec((1, C_Q_RANK)), _const_spec((C_Q_RANK, HALF)),
                  _const_spec((1, C_KV_RANK)), _const_spec((C_KV_RANK, HALF)), _const_spec((HALF, C_KV_RANK)),
                  tab, tab],
        out_specs=[tok(HALF), tok(HALF), vt_spec(KV_TILE), tok(HALF), tok(HALF), vt_spec(SB_KV_TILE)],
        out_shape=[sd((bsz, s, HALF), BF16), sd((bsz, s, HALF), BF16),
                   sd((bsz, s // KV_TILE, HALF, KV_TILE), BF16),
                   sd((bsz, s, HALF), BF16), sd((bsz, s, HALF), BF16),
                   sd((bsz, s // SB_KV_TILE, HALF, SB_KV_TILE), BF16)],
        compiler_params=_params("arbitrary", "arbitrary"),
        name="proj_odd",
    )(x, g.reshape(1, D_MODEL), wm, wdvt, q_norm_g.reshape(1, C_Q_RANK), wuq,
      kv_norm_g.reshape(1, C_KV_RANK), wukvk, wukvvt, cos, sin)


def kernel(x, mix_norm_g, ffn_norm_g, final_norm_g, ab_w_in, a_lambda_q1, a_lambda_k1, a_lambda_q2,
           a_lambda_k2, a_subln_g, ab_w_out, cd_w_in, c_q_norm_g, c_w_uq, c_kv_norm_g, c_w_ukv, cd_w_out,
           ffn_w_up, ffn_conv_w, ffn_conv_b, ffn_w_down):
    depth = mix_norm_g.shape[0]
    seq = x.shape[1]
    cos_ab, sin_ab, cos_c, sin_c = _rope_tables(seq)
    for layer in range(depth):
        j = layer // 2
        if layer % 2 == 0:
            lam_init = 0.8 - 0.6 * math.exp(-0.3 * layer)
            qa, ka, vat, qb, kb, vb = _proj_even(x, mix_norm_g[layer], ab_w_in[j], cos_ab, sin_ab)
            o1 = _flash_attention(qa, ka, vat, n_maps=2, lam_init=lam_init,
                                  lam_params=(a_lambda_q1[j], a_lambda_k1[j], a_lambda_q2[j], a_lambda_k2[j]),
                                  subln_g=a_subln_g[j])
            o2 = _dilated_attention(qb, kb, vb)
            w_out = ab_w_out[j]
        else:
            qc, kc, vct, qd, kd, vdt = _proj_odd(x, mix_norm_g[layer], cd_w_in[j], c_q_norm_g[j], c_w_uq[j],
                                                 c_kv_norm_g[j], c_w_ukv[j], cos_c, sin_c)
            o1 = _flash_attention(qc, kc, vct, n_maps=1)
            o2 = _stick_attention(qd, kd, vdt)
            w_out = cd_w_out[j]
        x = _post(x, o1, o2, w_out, ffn_norm_g[layer], ffn_w_up[layer], ffn_conv_w[layer], ffn_conv_b[layer],
                  ffn_w_down[layer], final_norm_g if layer == depth - 1 else None)
    return x
```
